```python
import math
import jax, jax.numpy as jnp
from jax import lax
import numpy as np

D_MODEL = 2048
BATCH = 2
SEQ = 8192
DEPTH = 1

EPS = 1e-6
NEG = -1e30
BLOCK = 128
SWA_HEADS = 16
SWA_KV_HEADS = 4
SWA_HEAD_DIM = 64
SWA_WINDOW = 128
SWA_Q_W = SWA_HEADS * SWA_HEAD_DIM
SWA_KV_W = SWA_KV_HEADS * SWA_HEAD_DIM
MLA_HEADS = 8
MLA_Q_RANK = 512
MLA_KV_RANK = 256
MLA_NOPE_DIM = 128
MLA_ROPE_DIM = 64
MLA_V_DIM = 128
MLA_V_W = MLA_HEADS * MLA_V_DIM
ROPE_THETA = 10000.0
PEER_HEADS = 8
PEER_N_KEYS = 128
PEER_N_EXPERTS = PEER_N_KEYS * PEER_N_KEYS
PEER_QUERY_DIM = 256
PEER_HALF = PEER_QUERY_DIM // 2
PEER_TOPK = 16
PEER_TOKEN_CHUNK = 128
IN_WIDTHS = [SWA_Q_W, SWA_KV_W, SWA_KV_W, MLA_Q_RANK, MLA_KV_RANK, MLA_ROPE_DIM, D_MODEL, D_MODEL]
IN_WIDTH = sum(IN_WIDTHS)
IN_OFFSETS = [int(o) for o in np.cumsum(IN_WIDTHS)[:-1]]

kernel_name = "hybrid_swa_mla_peer_block"


def rms_norm(x, g):
    xf = x.astype(jnp.float32)
    y = xf * lax.rsqrt(jnp.mean(xf * xf, axis=-1, keepdims=True) + EPS)
    return (y * g.astype(jnp.float32)).astype(x.dtype)


def alibi_slopes(n_heads):
    return jnp.asarray([2.0 ** (-8.0 * (h + 1) / n_heads) for h in range(n_heads)], dtype=jnp.float32)


def rope_tables(pos, dim):
    inv = 1.0 / (ROPE_THETA ** (jnp.arange(0, dim, 2, dtype=jnp.float32) / dim))
    ang = pos.astype(jnp.float32)[..., None] * inv
    return jnp.cos(ang), jnp.sin(ang)


def apply_rope(t, cos, sin):
    tf = t.astype(jnp.float32)
    t1, t2 = jnp.split(tf, 2, axis=-1)
    out = jnp.concatenate([t1 * cos - t2 * sin, t1 * sin + t2 * cos], axis=-1)
    return out.astype(t.dtype)


def with_prev_block(t):
    prev = jnp.concatenate([jnp.zeros_like(t[:, :1]), t[:, :-1]], axis=1)
    return jnp.concatenate([prev, t], axis=2)


def swa_attention(q, k, v, pos, sinks):
    B, S = q.shape[:2]
    nb = S // BLOCK
    G = SWA_HEADS // SWA_KV_HEADS
    qb = q.reshape(B, nb, BLOCK, SWA_KV_HEADS, G, SWA_HEAD_DIM)
    kk = with_prev_block(k.reshape(B, nb, BLOCK, SWA_KV_HEADS, SWA_HEAD_DIM))
    vv = with_prev_block(v.reshape(B, nb, BLOCK, SWA_KV_HEADS, SWA_HEAD_DIM))
    pq = pos.astype(jnp.float32).reshape(B, nb, BLOCK)
    pk = with_prev_block(pq)
    logits = jnp.einsum('bnqhgd,bnkhd->bnhgqk', qb, kk).astype(jnp.float32) * (SWA_HEAD_DIM ** -0.5)
    dist = jnp.abs(pq[:, :, :, None] - pk[:, :, None, :])
    slopes = alibi_slopes(SWA_HEADS).reshape(SWA_KV_HEADS, G, 1, 1)
    logits = logits - slopes * dist[:, :, None, None]
    qi = jnp.arange(BLOCK)[:, None]
    kj = jnp.arange(2 * BLOCK)[None, :]
    rel = BLOCK + qi - kj
    band = (rel >= 0) & (rel < SWA_WINDOW)
    valid_prev = (jnp.arange(nb)[:, None, None] > 0) | (kj >= BLOCK)[None]
    mask = band[None] & valid_prev
    logits = jnp.where(mask[None, :, None, None], logits, NEG)
    sink = sinks.astype(jnp.float32).reshape(1, 1, SWA_KV_HEADS, G, 1, 1)
    m = jnp.maximum(jnp.max(logits, axis=-1, keepdims=True), sink)
    p = jnp.exp(logits - m)
    probs = (p / (jnp.sum(p, axis=-1, keepdims=True) + jnp.exp(sink - m))).astype(v.dtype)
    out = jnp.einsum('bnhgqk,bnkhd->bnqhgd', probs, vv)
    return out.reshape(B, S, SWA_Q_W)


def mla_attention(c_q, c_kv, k_rope_raw, pos, g_cq, w_uq, g_ckv, w_ukv):
    B, S = c_q.shape[:2]
    nb = S // BLOCK
    q = (rms_norm(c_q, g_cq) @ w_uq).reshape(B, S, MLA_HEADS, MLA_NOPE_DIM + MLA_ROPE_DIM)
    q_nope, q_rope = q[..., :MLA_NOPE_DIM], q[..., MLA_NOPE_DIM:]
    kv = (rms_norm(c_kv, g_ckv) @ w_ukv).reshape(B, S, MLA_HEADS, MLA_NOPE_DIM + MLA_V_DIM)
    k_nope, v = kv[..., :MLA_NOPE_DIM], kv[..., MLA_NOPE_DIM:]
    cos, sin = rope_tables(pos, MLA_ROPE_DIM)
    q_rope = apply_rope(q_rope, cos[:, :, None], sin[:, :, None])
    k_rope = apply_rope(k_rope_raw, cos, sin)
    scale = (MLA_NOPE_DIM + MLA_ROPE_DIM) ** -0.5
    qn_b = q_nope.reshape(B, nb, BLOCK, MLA_HEADS, MLA_NOPE_DIM).swapaxes(0, 1)
    qr_b = q_rope.reshape(B, nb, BLOCK, MLA_HEADS, MLA_ROPE_DIM).swapaxes(0, 1)
    key_idx = jnp.arange(S)

    def one_block(args):
        qn, qr, i = args
        s = (jnp.einsum('bqhd,bkhd->bhqk', qn, k_nope)
             + jnp.einsum('bqhr,bkr->bhqk', qr, k_rope)).astype(jnp.float32) * scale
        q_idx = i * BLOCK + jnp.arange(BLOCK)
        s = jnp.where(key_idx[None, :] <= q_idx[:, None], s, NEG)
        p = jax.nn.softmax(s, axis=-1).astype(v.dtype)
        return jnp.einsum('bhqk,bkhd->bqhd', p, v)

    out = lax.map(one_block, (qn_b, qr_b, jnp.arange(nb)))
    return out.swapaxes(0, 1).reshape(B, S, MLA_V_W)


def peer_ffn(h, w_q, keys, u, v):
    B, S, D = h.shape
    q = (h @ w_q).reshape(B, S, PEER_HEADS, 2, PEER_HALF)
    s = jnp.einsum('bshpd,hpnd->bshpn', q, keys).astype(jnp.float32)
    top_s, top_i = lax.top_k(s, PEER_TOPK)
    cand_s = (top_s[..., 0, :, None] + top_s[..., 1, None, :]).reshape(B, S, PEER_HEADS, PEER_TOPK * PEER_TOPK)
    cand_i = (top_i[..., 0, :, None] * PEER_N_KEYS + top_i[..., 1, None, :]).reshape(B, S, PEER_HEADS, PEER_TOPK * PEER_TOPK)
    best_s, best_pos = lax.top_k(cand_s, PEER_TOPK)
    expert = jnp.take_along_axis(cand_i, best_pos, axis=-1)
    gate = jax.nn.softmax(best_s, axis=-1)
    T = B * S
    HK = PEER_HEADS * PEER_TOPK
    nc = T // PEER_TOKEN_CHUNK
    xt = h.reshape(nc, PEER_TOKEN_CHUNK, D)
    it = expert.reshape(nc, PEER_TOKEN_CHUNK, HK)
    gt = gate.reshape(nc, PEER_TOKEN_CHUNK, HK)

    def one_chunk(args):
        xc, ic, gc = args
        a = jnp.einsum('cd,ced->ce', xc, u[ic])
        w = (jax.nn.gelu(a.astype(jnp.float32), approximate=False) * gc).astype(xc.dtype)
        return jnp.einsum('ce,ced->cd', w, v[ic])

    out = lax.map(one_chunk, (xt, it, gt))
    return out.reshape(B, S, D)


def setup_inputs(seed: int = 0) -> dict:
    key = jax.random.key(seed)
    ks = jax.random.split(key, 20)
    f32 = jnp.float32

    def nrm(k, shape, scale):
        return jax.random.normal(k, shape, f32) * scale

    def gain(k, shape):
        return 1.0 + 0.02 * jax.random.normal(k, shape, f32)

    x = jax.random.normal(ks[0], (BATCH, SEQ, D_MODEL), f32)
    offset = jax.random.randint(ks[1], (BATCH, 1), 0, 1024, dtype=jnp.int32)
    positions = (offset + jnp.arange(SEQ, dtype=jnp.int32)[None, :]).astype(jnp.int32)
    return {
        "x": x,
        "positions": positions,
        "g_mix": gain(ks[2], (DEPTH, D_MODEL)),
        "w_in": nrm(ks[3], (DEPTH, D_MODEL, IN_WIDTH), D_MODEL ** -0.5),
        "sinks": nrm(ks[4], (DEPTH, SWA_HEADS), 0.5),
        "g_cq": gain(ks[5], (DEPTH, MLA_Q_RANK)),
        "w_uq": nrm(ks[6], (DEPTH, MLA_Q_RANK, MLA_HEADS * (MLA_NOPE_DIM + MLA_ROPE_DIM)), MLA_Q_RANK ** -0.5),
        "g_ckv": gain(ks[7], (DEPTH, MLA_KV_RANK)),
        "w_ukv": nrm(ks[8], (DEPTH, MLA_KV_RANK, MLA_HEADS * (MLA_NOPE_DIM + MLA_V_DIM)), MLA_KV_RANK ** -0.5),
        "w_a_proj": nrm(ks[9], (DEPTH, SWA_Q_W, D_MODEL), SWA_Q_W ** -0.5),
        "w_b_proj": nrm(ks[10], (DEPTH, MLA_V_W, D_MODEL), MLA_V_W ** -0.5),
        "w_o": nrm(ks[11], (DEPTH, D_MODEL, D_MODEL), D_MODEL ** -0.5),
        "g_ffn": gain(ks[12], (DEPTH, D_MODEL)),
        "w_peer_q": nrm(ks[13], (DEPTH, D_MODEL, PEER_HEADS * PEER_QUERY_DIM), D_MODEL ** -0.5),
        "peer_keys": nrm(ks[14], (DEPTH, PEER_HEADS, 2, PEER_N_KEYS, PEER_HALF), PEER_HALF ** -0.5),
        "peer_u": nrm(ks[15], (DEPTH, PEER_N_EXPERTS, D_MODEL), D_MODEL ** -0.5),
        "peer_v": nrm(ks[16], (DEPTH, PEER_N_EXPERTS, D_MODEL), (PEER_HEADS * PEER_TOPK) ** -0.5),
        "g_final": gain(ks[17], (D_MODEL,)),
    }


def reference(x, positions, g_mix, w_in, sinks, g_cq, w_uq, g_ckv, w_ukv, w_a_proj, w_b_proj, w_o,
              g_ffn, w_peer_q, peer_keys, peer_u, peer_v, g_final):
    for l in range(DEPTH):
        h = rms_norm(x, g_mix[l])
        z = h @ w_in[l]
        q_a, k_a, v_a, c_q, c_kv, k_r, gate_a, gate_b = jnp.split(z, IN_OFFSETS, axis=-1)
        y_a = swa_attention(q_a, k_a, v_a, positions, sinks[l]) @ w_a_proj[l]
        y_b = mla_attention(c_q, c_kv, k_r, positions, g_cq[l], w_uq[l], g_ckv[l], w_ukv[l]) @ w_b_proj[l]
        mixed = jax.nn.sigmoid(gate_a) * y_a + jax.nn.sigmoid(gate_b) * y_b
        x = x + mixed @ w_o[l]
        x = x + peer_ffn(rms_norm(x, g_ffn[l]), w_peer_q[l], peer_keys[l], peer_u[l], peer_v[l])
    return rms_norm(x, g_final)
```

```python
import functools
import math

import numpy as np
import jax
import jax.numpy as jnp
from jax import lax
from jax.experimental import pallas as pl
from jax.experimental.pallas import tpu as pltpu

F32 = jnp.float32
BF16 = jnp.bfloat16

EPS = 1e-6
NEG = -1e30
BLOCK = 128
SWA_HEADS = 16
SWA_KV_HEADS = 4
SWA_GROUP = SWA_HEADS // SWA_KV_HEADS
SWA_HEAD_DIM = 64
SWA_WINDOW = 128
SWA_Q_W = SWA_HEADS * SWA_HEAD_DIM
SWA_KV_W = SWA_KV_HEADS * SWA_HEAD_DIM
MLA_HEADS = 8
MLA_Q_RANK = 512
MLA_KV_RANK = 256
MLA_NOPE_DIM = 128
MLA_ROPE_DIM = 64
MLA_QK_DIM = MLA_NOPE_DIM + MLA_ROPE_DIM
MLA_V_DIM = 128
MLA_V_W = MLA_HEADS * MLA_V_DIM
ROPE_THETA = 10000.0
PEER_HEADS = 8
PEER_N_KEYS = 128
PEER_N_EXPERTS = PEER_N_KEYS * PEER_N_KEYS
PEER_HALF = 128
PEER_TOPK = 16

V7X_VMEM_BYTES = 64 * 1024 * 1024
VMEM_LIMIT = V7X_VMEM_BYTES - 8 * 1024 * 1024


def _params(*semantics):
    return pltpu.CompilerParams(dimension_semantics=semantics, vmem_limit_bytes=VMEM_LIMIT)


def _rms(xf, g):
    y = xf * lax.rsqrt(jnp.mean(xf * xf, axis=-1, keepdims=True) + EPS)
    return y * g


def _dot(a, b):
    return jnp.dot(a, b, preferred_element_type=F32)


def _dot_nt(a, b):
    return lax.dot_general(a, b, (((1,), (1,)), ((), ())), preferred_element_type=F32)


ATTN_SPLITS = (SWA_Q_W, MLA_Q_RANK, SWA_KV_W, SWA_KV_W, MLA_KV_RANK, 2 * MLA_ROPE_DIM)
ATTN_W = sum(ATTN_SPLITS)


def _attn_inproj_kernel(x_ref, g_ref, w_ref, *out_refs):
    h = _rms(x_ref[...], g_ref[...]).astype(BF16)
    z = _dot(h, w_ref[...])
    off = 0
    for o_ref, width in zip(out_refs, ATTN_SPLITS):
        o_ref[...] = z[:, off:off + width].astype(o_ref.dtype)
        off += width


def _attn_inproj(x2, g, w_attn, tm=512):
    T, D = x2.shape
    return pl.pallas_call(
        _attn_inproj_kernel,
        grid=(T // tm,),
        in_specs=[
            pl.BlockSpec((tm, D), lambda i: (i, 0)),
            pl.BlockSpec((1, D), lambda i: (0, 0)),
            pl.BlockSpec((D, ATTN_W), lambda i: (0, 0)),
        ],
        out_specs=[pl.BlockSpec((tm, w), lambda i: (i, 0)) for w in ATTN_SPLITS],
        out_shape=[jax.ShapeDtypeStruct((T, w), BF16) for w in ATTN_SPLITS],
        compiler_params=_params("parallel"),
        name="attn_inproj",
    )(x2, g, w_attn)


SWA_TILE = 512
SWA_SUB = SWA_TILE // BLOCK


def _swa_kernel(tiles_per_seq, sinks_ref, q_ref, kc_ref, kp_ref, vc_ref, vp_ref, pc_ref, prc_ref, prp_ref,
                o_ref, kbuf, vbuf, pkbuf):
    i = pl.program_id(0)
    kbuf[0:BLOCK, :] = kp_ref[...]
    kbuf[BLOCK:, :] = kc_ref[...]
    vbuf[0:BLOCK, :] = vp_ref[...]
    vbuf[BLOCK:, :] = vc_ref[...]
    pkbuf[:, 0:BLOCK] = prp_ref[...]
    pkbuf[:, BLOCK:] = prc_ref[...]
    first_tile = (i % tiles_per_seq) == 0

    qi = lax.broadcasted_iota(jnp.int32, (BLOCK, 2 * BLOCK), 0)
    kj = lax.broadcasted_iota(jnp.int32, (BLOCK, 2 * BLOCK), 1)
    rel = BLOCK + qi - kj
    band = (rel >= 0) & (rel < SWA_WINDOW)
    scale = SWA_HEAD_DIM ** -0.5

    for c in range(SWA_SUB):
        r0 = c * BLOCK
        pq = pc_ref[r0:r0 + BLOCK, :]
        pk = pkbuf[:, r0:r0 + 2 * BLOCK]
        dist = jnp.abs(pq - pk)
        if c == 0:
            mask = band & (jnp.logical_not(first_tile) | (kj >= BLOCK))
        else:
            mask = band
        for h in range(SWA_HEADS):
            g = h // SWA_GROUP
            slope = 2.0 ** (-8.0 * (h + 1) / SWA_HEADS)
            qh = q_ref[r0:r0 + BLOCK, h * SWA_HEAD_DIM:(h + 1) * SWA_HEAD_DIM]
            kk = kbuf[r0:r0 + 2 * BLOCK, g * SWA_HEAD_DIM:(g + 1) * SWA_HEAD_DIM]
            vv = vbuf[r0:r0 + 2 * BLOCK, g * SWA_HEAD_DIM:(g + 1) * SWA_HEAD_DIM]
            logits = _dot_nt(qh, kk) * scale - slope * dist
            logits = jnp.where(mask, logits, NEG)
            sink = sinks_ref[h]
            m = jnp.maximum(jnp.max(logits, axis=-1, keepdims=True), sink)
            p = jnp.exp(logits - m)
            denom = jnp.sum(p, axis=-1, keepdims=True) + jnp.exp(sink - m)
            o = _dot(p.astype(BF16), vv) / denom
            o_ref[r0:r0 + BLOCK, h * SWA_HEAD_DIM:(h + 1) * SWA_HEAD_DIM] = o.astype(o_ref.dtype)


def _swa(q_a, k_a, v_a, pos_col, pos_row, sinks, seq):
    T = q_a.shape[0]
    tiles_per_seq = seq // SWA_TILE

    def prev_blk(i):
        return jnp.where(i % tiles_per_seq == 0, i * SWA_SUB, i * SWA_SUB - 1)

    return pl.pallas_call(
        functools.partial(_swa_kernel, tiles_per_seq),
        grid=(T // SWA_TILE,),
        in_specs=[
            pl.BlockSpec(memory_space=pltpu.SMEM),
            pl.BlockSpec((SWA_TILE, SWA_Q_W), lambda i: (i, 0)),
            pl.BlockSpec((SWA_TILE, SWA_KV_W), lambda i: (i, 0)),
            pl.BlockSpec((BLOCK, SWA_KV_W), lambda i: (prev_blk(i), 0)),
            pl.BlockSpec((SWA_TILE, SWA_KV_W), lambda i: (i, 0)),
            pl.BlockSpec((BLOCK, SWA_KV_W), lambda i: (prev_blk(i), 0)),
            pl.BlockSpec((SWA_TILE, 1), lambda i: (i, 0)),
            pl.BlockSpec((1, SWA_TILE), lambda i: (0, i)),
            pl.BlockSpec((1, BLOCK), lambda i: (0, prev_blk(i))),
        ],
        out_specs=pl.BlockSpec((SWA_TILE, SWA_Q_W), lambda i: (i, 0)),
        out_shape=jax.ShapeDtypeStruct((T, SWA_Q_W), BF16),
        scratch_shapes=[
            pltpu.VMEM((SWA_TILE + BLOCK, SWA_KV_W), BF16),
            pltpu.VMEM((SWA_TILE + BLOCK, SWA_KV_W), BF16),
            pltpu.VMEM((1, SWA_TILE + BLOCK), F32),
        ],
        compiler_params=_params("parallel"),
        name="swa",
    )(sinks, q_a, k_a, k_a, v_a, v_a, pos_col, pos_row, pos_row)


def _mla_prep_kernel(cq_ref, ckv_ref, kr_ref, pos_ref, inv_ref, gcq_ref, gckv_ref, wqn_ref, wqr_ref, wqs_ref,
                     wkv_ref, q_out, k_out, v_out):
    scale = MLA_QK_DIM ** -0.5
    ang = pos_ref[...] * inv_ref[...]
    cos = jnp.cos(ang)
    sin = jnp.sin(ang)
    cos4 = jnp.concatenate([cos] * 4, axis=1)
    sin4 = jnp.concatenate([sin] * 4, axis=1)

    cn = _rms(cq_ref[...].astype(F32), gcq_ref[...]).astype(BF16)
    qn = _dot(cn, wqn_ref[...])
    qr = _dot(cn, wqr_ref[...])
    qs = _dot(cn, wqs_ref[...])
    qrot = qr * cos4 + qs * sin4

    kvn = _rms(ckv_ref[...].astype(F32), gckv_ref[...]).astype(BF16)
    kv = _dot(kvn, wkv_ref[...])
    kr = kr_ref[...].astype(F32)
    krot = (kr[:, :MLA_ROPE_DIM] * cos[:, :MLA_ROPE_DIM]
            + kr[:, MLA_ROPE_DIM:] * sin[:, :MLA_ROPE_DIM]).astype(BF16)

    for h in range(MLA_HEADS):
        q_h = jnp.concatenate(
            [qn[:, h * MLA_NOPE_DIM:(h + 1) * MLA_NOPE_DIM], qrot[:, h * MLA_ROPE_DIM:(h + 1) * MLA_ROPE_DIM]],
            axis=1) * scale
        q_out[0, h] = q_h.astype(BF16)
        kn_h = kv[:, h * 256:h * 256 + MLA_NOPE_DIM].astype(BF16)
        k_out[0, h] = jnp.concatenate([kn_h, krot], axis=1)
        v_out[0, h] = kv[:, h * 256 + MLA_NOPE_DIM:(h + 1) * 256].astype(BF16)


def _mla_prep(c_q, c_kv, k_r2, pos_col, inv128, g_cq, g_ckv, wq_n, wq_r, wq_s, w_kv, batch, seq, tm=512):
    T = c_q.shape[0]
    tps = seq // tm
    const = lambda i: (0, 0)
    out_idx = lambda i: (i // tps, 0, i % tps, 0)
    return pl.pallas_call(
        _mla_prep_kernel,
        grid=(T // tm,),
        in_specs=[
            pl.BlockSpec((tm, MLA_Q_RANK), lambda i: (i, 0)),
            pl.BlockSpec((tm, MLA_KV_RANK), lambda i: (i, 0)),
            pl.BlockSpec((tm, 2 * MLA_ROPE_DIM), lambda i: (i, 0)),
            pl.BlockSpec((tm, 1), lambda i: (i, 0)),
            pl.BlockSpec((1, 128), const),
            pl.BlockSpec((1, MLA_Q_RANK), const),
            pl.BlockSpec((1, MLA_KV_RANK), const),
            pl.BlockSpec(wq_n.shape, const),
            pl.BlockSpec(wq_r.shape, const),
            pl.BlockSpec(wq_s.shape, const),
            pl.BlockSpec(w_kv.shape, const),
        ],
        out_specs=[
            pl.BlockSpec((1, MLA_HEADS, tm, MLA_QK_DIM), out_idx),
            pl.BlockSpec((1, MLA_HEADS, tm, MLA_QK_DIM), out_idx),
            pl.BlockSpec((1, MLA_HEADS, tm, MLA_V_DIM), out_idx),
        ],
        out_shape=[
            jax.ShapeDtypeStruct((batch, MLA_HEADS, seq, MLA_QK_DIM), BF16),
            jax.ShapeDtypeStruct((batch, MLA_HEADS, seq, MLA_QK_DIM), BF16),
            jax.ShapeDtypeStruct((batch, MLA_HEADS, seq, MLA_V_DIM), BF16),
        ],
        compiler_params=_params("parallel"),
        name="mla_prep",
    )(c_q, c_kv, k_r2, pos_col, inv128, g_cq, g_ckv, wq_n, wq_r, wq_s, w_kv)


MLA_TILE = 512


def _mla_flash_kernel(qi_ref, ki_ref, q_ref, k_ref, v_ref, o_ref, m_sc, l_sc, acc_sc):
    p_id = pl.program_id(2)
    qi = qi_ref[p_id]
    ki = ki_ref[p_id]

    @pl.when(ki == 0)
    def _():
        m_sc[...] = jnp.full(m_sc.shape, NEG, F32)
        l_sc[...] = jnp.zeros(l_sc.shape, F32)
        acc_sc[...] = jnp.zeros(acc_sc.shape, F32)

    def update(s):
        m_prev = m_sc[...]
        m_new = jnp.maximum(m_prev, jnp.max(s, axis=-1, keepdims=True))
        alpha = jnp.exp(m_prev - m_new)
        p = jnp.exp(s - m_new)
        l_sc[...] = alpha * l_sc[...] + jnp.sum(p, axis=-1, keepdims=True)
        acc_sc[...] = alpha * acc_sc[...] + _dot(p.astype(BF16), v_ref[0, 0])
        m_sc[...] = m_new

    @pl.when(ki < qi)
    def _():
        update(_dot_nt(q_ref[0, 0], k_ref[0, 0]))

    @pl.when(ki == qi)
    def _():
        s = _dot_nt(q_ref[0, 0], k_ref[0, 0])
        row = lax.broadcasted_iota(jnp.int32, s.shape, 0)
        col = lax.broadcasted_iota(jnp.int32, s.shape, 1)
        update(jnp.where(col <= row, s, NEG))
        o_ref[0] = (acc_sc[...] / l_sc[...]).astype(o_ref.dtype)


def _mla_flash(q, k, v):
    B, H, S, _ = q.shape
    t = MLA_TILE
    nq = S // t
    pairs = [(a, b) for a in range(nq) for b in range(a + 1)]
    qi_tab = jnp.asarray([a for a, _ in pairs], jnp.int32)
    ki_tab = jnp.asarray([b for _, b in pairs], jnp.int32)
    grid_spec = pltpu.PrefetchScalarGridSpec(
        num_scalar_prefetch=2,
        grid=(B, H, len(pairs)),
        in_specs=[
            pl.BlockSpec((1, 1, t, MLA_QK_DIM), lambda b, h, p, qi, ki: (b, h, qi[p], 0)),
            pl.BlockSpec((1, 1, t, MLA_QK_DIM), lambda b, h, p, qi, ki: (b, h, ki[p], 0)),
            pl.BlockSpec((1, 1, t, MLA_V_DIM), lambda b, h, p, qi, ki: (b, h, ki[p], 0)),
        ],
        out_specs=pl.BlockSpec((1, t, MLA_V_DIM), lambda b, h, p, qi, ki: (b, qi[p], h)),
        scratch_shapes=[
            pltpu.VMEM((t, 1), F32),
            pltpu.VMEM((t, 1), F32),
            pltpu.VMEM((t, MLA_V_DIM), F32),
        ],
    )
    return pl.pallas_call(
        _mla_flash_kernel,
        grid_spec=grid_spec,
        out_shape=jax.ShapeDtypeStruct((B, S, MLA_V_W), BF16),
        compiler_params=_params("parallel", "parallel", "arbitrary"),
        name="mla_flash",
    )(qi_tab, ki_tab, q, k, v)


def _mix_kernel(x_ref, g_ref, ya_in_ref, yb_in_ref, wga_ref, wgb_ref, wa_ref, wb_ref, wo_ref, o_ref, h_sc, acc_sc):
    n = pl.program_id(1)

    @pl.when(n == 0)
    def _():
        h_sc[...] = _rms(x_ref[...], g_ref[...]).astype(BF16)
        acc_sc[...] = jnp.zeros(acc_sc.shape, F32)

    h = h_sc[...]
    ga = _dot(h, wga_ref[...])
    gb = _dot(h, wgb_ref[...])
    ya = _dot(ya_in_ref[...], wa_ref[...])
    yb = _dot(yb_in_ref[...], wb_ref[...])
    mixed = jax.nn.sigmoid(ga) * ya + jax.nn.sigmoid(gb) * yb
    acc_sc[...] += _dot(mixed.astype(BF16), wo_ref[...])

    @pl.when(n == pl.num_programs(1) - 1)
    def _():
        o_ref[...] = x_ref[...] + acc_sc[...]


def _mix(x2, g, swa_out, mla_out, w_ga, w_gb, w_a, w_b, w_o, tm=512, tn=512):
    T, D = x2.shape
    return pl.pallas_call(
        _mix_kernel,
        grid=(T // tm, D // tn),
        in_specs=[
            pl.BlockSpec((tm, D), lambda i, n: (i, 0)),
            pl.BlockSpec((1, D), lambda i, n: (0, 0)),
            pl.BlockSpec((tm, SWA_Q_W), lambda i, n: (i, 0)),
            pl.BlockSpec((tm, MLA_V_W), lambda i, n: (i, 0)),
            pl.BlockSpec((D, tn), lambda i, n: (0, n)),
            pl.BlockSpec((D, tn), lambda i, n: (0, n)),
            pl.BlockSpec((SWA_Q_W, tn), lambda i, n: (0, n)),
            pl.BlockSpec((MLA_V_W, tn), lambda i, n: (0, n)),
            pl.BlockSpec((tn, D), lambda i, n: (n, 0)),
        ],
        out_specs=pl.BlockSpec((tm, D), lambda i, n: (i, 0)),
        out_shape=jax.ShapeDtypeStruct((T, D), F32),
        scratch_shapes=[pltpu.VMEM((tm, D), BF16), pltpu.VMEM((tm, D), F32)],
        compiler_params=_params("parallel", "arbitrary"),
        name="mix",
    )(x2, g, swa_out, mla_out, w_ga, w_gb, w_a, w_b, w_o)


PEER_LANES = 128
CAND_ROWS = 16 + 7 * 8 + 8


def _extract_top(s, iters):
    n_rows = s.shape[0]
    rows = lax.broadcasted_iota(jnp.int32, s.shape, 0)
    rank = jnp.full(s.shape, float(iters), F32)
    vals = []
    for r in range(iters):
        m = jnp.max(s, axis=0, keepdims=True)
        idx = jnp.min(jnp.where(s == m, rows, n_rows), axis=0, keepdims=True)
        sel = rows == idx
        rank = jnp.where(sel, float(r), rank)
        s = jnp.where(sel, -jnp.inf, s)
        vals.append(m)
    return rank, vals


def _peer_route_kernel(x_ref, g_ref, wq_ref, keys_ref, h_out, r2_out, n1_out, e1_out, e2_out, q_sc):
    tm = x_ref.shape[0]
    h = _rms(x_ref[...], g_ref[...]).astype(BF16)
    h_out[...] = h
    q_sc[...] = _dot(h, wq_ref[...]).astype(BF16)

    def head_body(hd, carry):
        c0 = pl.multiple_of(hd * 2 * PEER_HALF, 2 * PEER_HALF)
        for ch in range(tm // PEER_LANES):
            t0 = ch * PEER_LANES
            q1 = q_sc[t0:t0 + PEER_LANES, pl.ds(c0, PEER_HALF)]
            q2 = q_sc[t0:t0 + PEER_LANES, pl.ds(c0 + PEER_HALF, PEER_HALF)]
            s1 = _dot_nt(keys_ref[2 * hd], q1)
            s2 = _dot_nt(keys_ref[2 * hd + 1], q2)
            r1, v1 = _extract_top(s1, PEER_TOPK)
            r2, v2 = _extract_top(s2, PEER_TOPK)
            v2_lo = jnp.concatenate(v2[:8], axis=0)
            v2_all = jnp.concatenate(v2, axis=0)
            slabs = [v1[0] + v2_all]
            for a in range(1, 8):
                slabs.append(v1[a] + v2_lo)
            slabs.append(jnp.concatenate(v1[8:], axis=0) + v2[0])
            cand = jnp.concatenate(slabs, axis=0)
            rc, vc = _extract_top(cand, PEER_TOPK)
            picked = jnp.where(rc < float(PEER_TOPK), 1.0, 0.0)
            z = jnp.ones_like(vc[0])
            for r in range(1, PEER_TOPK):
                z = z + jnp.exp(vc[r] - vc[0])
            n1 = jnp.zeros_like(r1)
            n1 = jnp.where(r1 == 0.0, jnp.sum(picked[0:16], axis=0, keepdims=True), n1)
            for a in range(1, 8):
                n_a = jnp.sum(picked[8 + 8 * a:16 + 8 * a], axis=0, keepdims=True)
                n1 = jnp.where(r1 == float(a), n_a, n1)
            for a in range(8, 16):
                n1 = jnp.where(r1 == float(a), picked[64 + a:65 + a], n1)
            r2_out[hd, :, t0:t0 + PEER_LANES] = r2
            n1_out[hd, :, t0:t0 + PEER_LANES] = n1
            e1_out[hd, :, t0:t0 + PEER_LANES] = jnp.exp(s1 - v1[0])
            e2_out[hd, :, t0:t0 + PEER_LANES] = jnp.exp(s2 - v2[0]) / z
        return carry

    lax.fori_loop(0, PEER_HEADS, head_body, 0)


def _peer_route(x1, g, w_q, keys, tm=256):
    T, D = x1.shape
    side = jax.ShapeDtypeStruct((PEER_HEADS, PEER_N_KEYS, T), F32)
    side_spec = pl.BlockSpec((PEER_HEADS, PEER_N_KEYS, tm), lambda i: (0, 0, i))
    return pl.pallas_call(
        _peer_route_kernel,
        grid=(T // tm,),
        in_specs=[
            pl.BlockSpec((tm, D), lambda i: (i, 0)),
            pl.BlockSpec((1, D), lambda i: (0, 0)),
            pl.BlockSpec(w_q.shape, lambda i: (0, 0)),
            pl.BlockSpec(keys.shape, lambda i: (0, 0, 0)),
        ],
        out_specs=[pl.BlockSpec((tm, D), lambda i: (i, 0)), side_spec, side_spec, side_spec, side_spec],
        out_shape=[jax.ShapeDtypeStruct((T, D), BF16), side, side, side, side],
        scratch_shapes=[pltpu.VMEM((tm, w_q.shape[1]), BF16)],
        compiler_params=_params("parallel"),
        name="peer_route",
    )(x1, g, w_q, keys)


def _peer_dense_kernel(h_ref, u_ref, vt_ref, r2_ref, n1_ref, e1_ref, e2_ref, o_ref, acc_sc):
    e = pl.program_id(1)
    te = u_ref.shape[0]

    @pl.when(e == 0)
    def _():
        acc_sc[...] = jnp.zeros(acc_sc.shape, F32)

    a_t = _dot_nt(u_ref[...], h_ref[...])
    act = 0.5 * a_t * (1.0 + lax.erf(a_t * math.sqrt(0.5)))
    w_rows = []
    for ii in range(te // PEER_N_KEYS):
        i = e * (te // PEER_N_KEYS) + ii
        gate = jnp.zeros((PEER_N_KEYS, h_ref.shape[0]), F32)
        for hd in range(PEER_HEADS):
            n1 = n1_ref[hd, pl.ds(i, 1), :]
            e1 = e1_ref[hd, pl.ds(i, 1), :]
            gate = gate + jnp.where(r2_ref[hd] < n1, e2_ref[hd] * e1, 0.0)
        w_rows.append((act[ii * PEER_N_KEYS:(ii + 1) * PEER_N_KEYS] * gate).astype(BF16))
    w = jnp.concatenate(w_rows, axis=0) if len(w_rows) > 1 else w_rows[0]
    acc_sc[...] += _dot(vt_ref[...], w)

    @pl.when(e == pl.num_programs(1) - 1)
    def _():
        o_ref[...] = acc_sc[...].T


def _peer_dense(h2, u_bf, vt_bf, r2, n1, e1, e2, tm=512, te=256):
    T, D = h2.shape
    side_spec = pl.BlockSpec((PEER_HEADS, PEER_N_KEYS, tm), lambda i, e: (0, 0, i))
    return pl.pallas_call(
        _peer_dense_kernel,
        grid=(T // tm, PEER_N_EXPERTS // te),
        in_specs=[
            pl.BlockSpec((tm, D), lambda i, e: (i, 0)),
            pl.BlockSpec((te, D), lambda i, e: (e, 0)),
            pl.BlockSpec((D, te), lambda i, e: (0, e)),
            side_spec, side_spec, side_spec, side_spec,
        ],
        out_specs=pl.BlockSpec((tm, D), lambda i, e: (i, 0)),
        out_shape=jax.ShapeDtypeStruct((T, D), F32),
        scratch_shapes=[pltpu.VMEM((D, tm), F32)],
        compiler_params=_params("parallel", "arbitrary"),
        name="peer_dense",
    )(h2, u_bf, vt_bf, r2, n1, e1, e2)


def _final_kernel(x_ref, p_ref, g_ref, o_ref):
    o_ref[...] = _rms(x_ref[...] + p_ref[...], g_ref[...])


def _final_norm(x1, peer_out, g, tm=512):
    T, D = x1.shape
    return pl.pallas_call(
        _final_kernel,
        grid=(T // tm,),
        in_specs=[
            pl.BlockSpec((tm, D), lambda i: (i, 0)),
            pl.BlockSpec((tm, D), lambda i: (i, 0)),
            pl.BlockSpec((1, D), lambda i: (0, 0)),
        ],
        out_specs=pl.BlockSpec((tm, D), lambda i: (i, 0)),
        out_shape=jax.ShapeDtypeStruct((T, D), F32),
        compiler_params=_params("parallel"),
        name="final_norm",
    )(x1, peer_out, g)


def _layer(x2, pos_col, pos_row, batch, seq, g_mix, w_in, sinks, g_cq, w_uq, g_ckv, w_ukv, w_a_proj, w_b_proj,
           w_o, g_ffn, w_peer_q, peer_keys, peer_u, peer_v):
    D = x2.shape[1]
    o = np.cumsum([0, SWA_Q_W, SWA_KV_W, SWA_KV_W, MLA_Q_RANK, MLA_KV_RANK, MLA_ROPE_DIM, D, D])
    w_qa, w_ka, w_va, w_cq, w_ckv, w_kr, w_ga, w_gb = [w_in[:, o[j]:o[j + 1]] for j in range(8)]
    half = MLA_ROPE_DIM // 2
    w_kr_swapped = jnp.concatenate([-w_kr[:, half:], w_kr[:, :half]], axis=1)
    w_attn = jnp.concatenate([w_qa, w_cq, w_ka, w_va, w_ckv, w_kr, w_kr_swapped], axis=1).astype(BF16)

    q_a, c_q, k_a, v_a, c_kv, k_r2 = _attn_inproj(x2, g_mix.reshape(1, D), w_attn)
    swa_out = _swa(q_a, k_a, v_a, pos_col, pos_row, sinks.astype(F32), seq)

    w_uq3 = w_uq.reshape(MLA_Q_RANK, MLA_HEADS, MLA_QK_DIM)
    wq_n = w_uq3[:, :, :MLA_NOPE_DIM].reshape(MLA_Q_RANK, MLA_HEADS * MLA_NOPE_DIM).astype(BF16)
    t1 = w_uq3[:, :, MLA_NOPE_DIM:MLA_NOPE_DIM + half]
    t2 = w_uq3[:, :, MLA_NOPE_DIM + half:]
    wq_r = jnp.concatenate([t1, t2], axis=2).reshape(MLA_Q_RANK, MLA_HEADS * MLA_ROPE_DIM).astype(BF16)
    wq_s = jnp.concatenate([-t2, t1], axis=2).reshape(MLA_Q_RANK, MLA_HEADS * MLA_ROPE_DIM).astype(BF16)
    inv = 1.0 / (ROPE_THETA ** (jnp.arange(0, MLA_ROPE_DIM, 2, dtype=F32) / MLA_ROPE_DIM))
    inv128 = jnp.tile(inv, 128 // half).reshape(1, 128)
    q, k, v = _mla_prep(c_q, c_kv, k_r2, pos_col, inv128, g_cq.reshape(1, -1), g_ckv.reshape(1, -1),
                        wq_n, wq_r, wq_s, w_ukv.astype(BF16), batch, seq)
    mla_out = _mla_flash(q, k, v).reshape(batch * seq, MLA_V_W)

    x1 = _mix(x2, g_mix.reshape(1, D), swa_out, mla_out, w_ga.astype(BF16), w_gb.astype(BF16),
              w_a_proj.astype(BF16), w_b_proj.astype(BF16), w_o.astype(BF16))

    keys = peer_keys.reshape(PEER_HEADS * 2, PEER_N_KEYS, PEER_HALF).astype(BF16)
    h2, r2, n1, e1, e2 = _peer_route(x1, g_ffn.reshape(1, D), w_peer_q.astype(BF16), keys)
    peer_out = _peer_dense(h2, peer_u.astype(BF16), peer_v.T.astype(BF16), r2, n1, e1, e2)
    return x1, peer_out


def kernel(x, positions, g_mix, w_in, sinks, g_cq, w_uq, g_ckv, w_ukv, w_a_proj, w_b_proj, w_o, g_ffn, w_peer_q,
           peer_keys, peer_u, peer_v, g_final):
    batch, seq, D = x.shape
    depth = g_mix.shape[0]
    x2 = x.reshape(batch * seq, D)
    pos_f = positions.astype(F32)
    pos_col = pos_f.reshape(batch * seq, 1)
    pos_row = pos_f.reshape(1, batch * seq)
    peer_out = None
    for l in range(depth):
        if peer_out is not None:
            x2 = x2 + peer_out
        x2, peer_out = _layer(x2, pos_col, pos_row, batch, seq, g_mix[l], w_in[l], sinks[l], g_cq[l], w_uq[l],
                              g_ckv[l], w_ukv[l], w_a_proj[l], w_b_proj[l], w_o[l], g_ffn[l], w_peer_q[l],
                              peer_keys[l], peer_u[l], peer_v[l])
    out = _final_norm(x2, peer_out, g_final.reshape(1, D))
    return out.reshape(batch, seq, D)
```

```python
import functools
import math

import numpy as np
import jax
import jax.numpy as jnp
from jax import lax
from jax.experimental import pallas as pl
from jax.experimental.pallas import tpu as pltpu

F32 = jnp.float32
BF16 = jnp.bfloat16

EPS = 1e-6
NEG = -1e30
BLOCK = 128
SWA_HEADS = 16
SWA_KV_HEADS = 4
SWA_GROUP = SWA_HEADS // SWA_KV_HEADS
SWA_HEAD_DIM = 64
SWA_WINDOW = 128
SWA_Q_W = SWA_HEADS * SWA_HEAD_DIM
SWA_KV_W = SWA_KV_HEADS * SWA_HEAD_DIM
MLA_HEADS = 8
MLA_Q_RANK = 512
MLA_KV_RANK = 256
MLA_NOPE_DIM = 128
MLA_ROPE_DIM = 64
MLA_QK_DIM = MLA_NOPE_DIM + MLA_ROPE_DIM
MLA_V_DIM = 128
MLA_V_W = MLA_HEADS * MLA_V_DIM
ROPE_THETA = 10000.0
PEER_HEADS = 8
PEER_N_KEYS = 128
PEER_N_EXPERTS = PEER_N_KEYS * PEER_N_KEYS
PEER_HALF = 128
PEER_TOPK = 16

V7X_VMEM_BYTES = 64 * 1024 * 1024
VMEM_LIMIT = V7X_VMEM_BYTES - 8 * 1024 * 1024


def _params(*semantics):
    return pltpu.CompilerParams(dimension_semantics=semantics, vmem_limit_bytes=VMEM_LIMIT)


def _rms(xf, g):
    y = xf * lax.rsqrt(jnp.mean(xf * xf, axis=-1, keepdims=True) + EPS)
    return y * g


def _dot(a, b):
    return jnp.dot(a, b, preferred_element_type=F32)


def _dot_nt(a, b):
    return lax.dot_general(a, b, (((1,), (1,)), ((), ())), preferred_element_type=F32)


ATTN_SPLITS = (SWA_Q_W, MLA_Q_RANK, SWA_KV_W, SWA_KV_W, MLA_KV_RANK, 2 * MLA_ROPE_DIM)
ATTN_W = sum(ATTN_SPLITS)
LOG2E = math.log2(math.e)


def _attn_inproj_kernel(x_ref, g_ref, w_ref, q_out, cq_out, k_out, v_out, ckv_out, kr_out):
    h = _rms(x_ref[...], g_ref[...]).astype(BF16)
    z = _dot(h, w_ref[...]).astype(BF16)
    d = SWA_HEAD_DIM
    off = 0
    for hd in range(SWA_HEADS):
        q_out[hd] = z[:, off + hd * d:off + (hd + 1) * d]
    off += SWA_Q_W
    cq_out[...] = z[:, off:off + MLA_Q_RANK]
    off += MLA_Q_RANK
    for hd in range(SWA_KV_HEADS):
        k_out[hd] = z[:, off + hd * d:off + (hd + 1) * d]
    off += SWA_KV_W
    for hd in range(SWA_KV_HEADS):
        v_out[hd] = z[:, off + hd * d:off + (hd + 1) * d]
    off += SWA_KV_W
    ckv_out[...] = z[:, off:off + MLA_KV_RANK]
    off += MLA_KV_RANK
    kr_out[...] = z[:, off:off + 2 * MLA_ROPE_DIM]


def _attn_inproj(x2, g, w_attn, tm=512):
    T, D = x2.shape
    d = SWA_HEAD_DIM
    row = lambda i: (i, 0)
    head_major = lambda i: (0, i, 0)
    return pl.pallas_call(
        _attn_inproj_kernel,
        grid=(T // tm,),
        in_specs=[
            pl.BlockSpec((tm, D), row),
            pl.BlockSpec((1, D), lambda i: (0, 0)),
            pl.BlockSpec((D, ATTN_W), lambda i: (0, 0)),
        ],
        out_specs=[
            pl.BlockSpec((SWA_HEADS, tm, d), head_major),
            pl.BlockSpec((tm, MLA_Q_RANK), row),
            pl.BlockSpec((SWA_KV_HEADS, tm, d), head_major),
            pl.BlockSpec((SWA_KV_HEADS, tm, d), head_major),
            pl.BlockSpec((tm, MLA_KV_RANK), row),
            pl.BlockSpec((tm, 2 * MLA_ROPE_DIM), row),
        ],
        out_shape=[
            jax.ShapeDtypeStruct((SWA_HEADS, T, d), BF16),
            jax.ShapeDtypeStruct((T, MLA_Q_RANK), BF16),
            jax.ShapeDtypeStruct((SWA_KV_HEADS, T, d), BF16),
            jax.ShapeDtypeStruct((SWA_KV_HEADS, T, d), BF16),
            jax.ShapeDtypeStruct((T, MLA_KV_RANK), BF16),
            jax.ShapeDtypeStruct((T, 2 * MLA_ROPE_DIM), BF16),
        ],
        compiler_params=_params("parallel"),
        name="attn_inproj",
    )(x2, g, w_attn)


SWA_TILE = 512
SWA_SUB = SWA_TILE // BLOCK


def _swa_kernel(tiles_per_seq, sinks_ref, q_ref, kc_ref, kp_ref, vc_ref, vp_ref, pc_ref, prc_ref, prp_ref,
                o_ref, kbuf, vbuf, pkbuf):
    i = pl.program_id(0)
    kbuf[:, 0:BLOCK, :] = kp_ref[...]
    kbuf[:, BLOCK:, :] = kc_ref[...]
    vbuf[:, 0:BLOCK, :] = vp_ref[...]
    vbuf[:, BLOCK:, :] = vc_ref[...]
    pkbuf[:, 0:BLOCK] = prp_ref[...]
    pkbuf[:, BLOCK:] = prc_ref[...]
    first_tile = (i % tiles_per_seq) == 0

    qi = lax.broadcasted_iota(jnp.int32, (BLOCK, 2 * BLOCK), 0)
    kj = lax.broadcasted_iota(jnp.int32, (BLOCK, 2 * BLOCK), 1)
    rel = BLOCK + qi - kj
    band = (rel >= 0) & (rel < SWA_WINDOW)
    d = SWA_HEAD_DIM

    for c in range(SWA_SUB):
        r0 = c * BLOCK
        pq = pc_ref[r0:r0 + BLOCK, :]
        pk = pkbuf[:, r0:r0 + 2 * BLOCK]
        dist = jnp.abs(pq - pk) * LOG2E
        if c == 0:
            mask = band & (jnp.logical_not(first_tile) | (kj >= BLOCK))
        else:
            mask = band
        for g in range(SWA_KV_HEADS):
            qs = jnp.concatenate([q_ref[g * SWA_GROUP + hh, r0:r0 + BLOCK, :] for hh in range(SWA_GROUP)], axis=0)
            logits = _dot_nt(qs, kbuf[g, r0:r0 + 2 * BLOCK, :])
            probs, denoms = [], []
            for hh in range(SWA_GROUP):
                h = g * SWA_GROUP + hh
                slope = 2.0 ** (-8.0 * (h + 1) / SWA_HEADS)
                lh = logits[hh * BLOCK:(hh + 1) * BLOCK] - slope * dist
                lh = jnp.where(mask, lh, NEG)
                sink = sinks_ref[h] * LOG2E
                m = jnp.maximum(jnp.max(lh, axis=-1, keepdims=True), sink)
                p = jnp.exp2(lh - m)
                denoms.append(jnp.sum(p, axis=-1, keepdims=True) + jnp.exp2(sink - m))
                probs.append(p.astype(BF16))
            o = _dot(jnp.concatenate(probs, axis=0), vbuf[g, r0:r0 + 2 * BLOCK, :])
            for hh in range(SWA_GROUP):
                h = g * SWA_GROUP + hh
                o_ref[r0:r0 + BLOCK, h * d:(h + 1) * d] = (o[hh * BLOCK:(hh + 1) * BLOCK] / denoms[hh]).astype(
                    o_ref.dtype)


def _swa(q_h, k_h, v_h, pos_col, pos_row, sinks, seq):
    T = q_h.shape[1]
    d = SWA_HEAD_DIM
    tiles_per_seq = seq // SWA_TILE

    def prev_blk(i):
        return jnp.where(i % tiles_per_seq == 0, i * SWA_SUB, i * SWA_SUB - 1)

    cur = lambda i: (0, i, 0)
    prev = lambda i: (0, prev_blk(i), 0)
    return pl.pallas_call(
        functools.partial(_swa_kernel, tiles_per_seq),
        grid=(T // SWA_TILE,),
        in_specs=[
            pl.BlockSpec(memory_space=pltpu.SMEM),
            pl.BlockSpec((SWA_HEADS, SWA_TILE, d), cur),
            pl.BlockSpec((SWA_KV_HEADS, SWA_TILE, d), cur),
            pl.BlockSpec((SWA_KV_HEADS, BLOCK, d), prev),
            pl.BlockSpec((SWA_KV_HEADS, SWA_TILE, d), cur),
            pl.BlockSpec((SWA_KV_HEADS, BLOCK, d), prev),
            pl.BlockSpec((SWA_TILE, 1), lambda i: (i, 0)),
            pl.BlockSpec((1, SWA_TILE), lambda i: (0, i)),
            pl.BlockSpec((1, BLOCK), lambda i: (0, prev_blk(i))),
        ],
        out_specs=pl.BlockSpec((SWA_TILE, SWA_Q_W), lambda i: (i, 0)),
        out_shape=jax.ShapeDtypeStruct((T, SWA_Q_W), BF16),
        scratch_shapes=[
            pltpu.VMEM((SWA_KV_HEADS, SWA_TILE + BLOCK, d), BF16),
            pltpu.VMEM((SWA_KV_HEADS, SWA_TILE + BLOCK, d), BF16),
            pltpu.VMEM((1, SWA_TILE + BLOCK), F32),
        ],
        compiler_params=_params("parallel"),
        name="swa",
    )(sinks, q_h, k_h, k_h, v_h, v_h, pos_col, pos_row, pos_row)


MLA_VP = 2 * MLA_V_DIM


def _mla_prep_kernel(cq_ref, ckv_ref, kr_ref, pos_ref, inv_ref, gcq_ref, gckv_ref, wqn_ref, wqr_ref, wqs_ref,
                     wkv_ref, q_out, k_out, v_out):
    scale = MLA_QK_DIM ** -0.5 * math.log2(math.e)
    ang = pos_ref[...] * inv_ref[...]
    cos = jnp.cos(ang)
    sin = jnp.sin(ang)
    cos4 = jnp.concatenate([cos] * 4, axis=1)
    sin4 = jnp.concatenate([sin] * 4, axis=1)

    cn = _rms(cq_ref[...].astype(F32), gcq_ref[...]).astype(BF16)
    qn = _dot(cn, wqn_ref[...])
    qr = _dot(cn, wqr_ref[...])
    qs = _dot(cn, wqs_ref[...])
    qrot = qr * cos4 + qs * sin4

    kvn = _rms(ckv_ref[...].astype(F32), gckv_ref[...]).astype(BF16)
    kv = _dot(kvn, wkv_ref[...])
    kr = kr_ref[...].astype(F32)
    krot = (kr[:, :MLA_ROPE_DIM] * cos[:, :MLA_ROPE_DIM]
            + kr[:, MLA_ROPE_DIM:] * sin[:, :MLA_ROPE_DIM]).astype(BF16)

    lane = lax.broadcasted_iota(jnp.int32, (kr.shape[0], MLA_VP - MLA_V_DIM), 1)
    ones_col = jnp.where(lane == 0, 1.0, 0.0).astype(BF16)
    for h in range(MLA_HEADS):
        q_h = jnp.concatenate(
            [qn[:, h * MLA_NOPE_DIM:(h + 1) * MLA_NOPE_DIM], qrot[:, h * MLA_ROPE_DIM:(h + 1) * MLA_ROPE_DIM]],
            axis=1) * scale
        q_out[0, h] = q_h.astype(BF16)
        kn_h = kv[:, h * 256:h * 256 + MLA_NOPE_DIM].astype(BF16)
        k_out[0, h] = jnp.concatenate([kn_h, krot], axis=1)
        v_out[0, h, :, :MLA_V_DIM] = kv[:, h * 256 + MLA_NOPE_DIM:(h + 1) * 256].astype(BF16)
        v_out[0, h, :, MLA_V_DIM:] = ones_col


def _mla_prep(c_q, c_kv, k_r2, pos_col, inv128, g_cq, g_ckv, wq_n, wq_r, wq_s, w_kv, batch, seq, tm=512):
    T = c_q.shape[0]
    tps = seq // tm
    const = lambda i: (0, 0)
    out_idx = lambda i: (i // tps, 0, i % tps, 0)
    return pl.pallas_call(
        _mla_prep_kernel,
        grid=(T // tm,),
        in_specs=[
            pl.BlockSpec((tm, MLA_Q_RANK), lambda i: (i, 0)),
            pl.BlockSpec((tm, MLA_KV_RANK), lambda i: (i, 0)),
            pl.BlockSpec((tm, 2 * MLA_ROPE_DIM), lambda i: (i, 0)),
            pl.BlockSpec((tm, 1), lambda i: (i, 0)),
            pl.BlockSpec((1, 128), const),
            pl.BlockSpec((1, MLA_Q_RANK), const),
            pl.BlockSpec((1, MLA_KV_RANK), const),
            pl.BlockSpec(wq_n.shape, const),
            pl.BlockSpec(wq_r.shape, const),
            pl.BlockSpec(wq_s.shape, const),
            pl.BlockSpec(w_kv.shape, const),
        ],
        out_specs=[
            pl.BlockSpec((1, MLA_HEADS, tm, MLA_QK_DIM), out_idx),
            pl.BlockSpec((1, MLA_HEADS, tm, MLA_QK_DIM), out_idx),
            pl.BlockSpec((1, MLA_HEADS, tm, MLA_VP), out_idx),
        ],
        out_shape=[
            jax.ShapeDtypeStruct((batch, MLA_HEADS, seq, MLA_QK_DIM), BF16),
            jax.ShapeDtypeStruct((batch, MLA_HEADS, seq, MLA_QK_DIM), BF16),
            jax.ShapeDtypeStruct((batch, MLA_HEADS, seq, MLA_VP), BF16),
        ],
        compiler_params=_params("parallel"),
        name="mla_prep",
    )(c_q, c_kv, k_r2, pos_col, inv128, g_cq, g_ckv, wq_n, wq_r, wq_s, w_kv)


MLA_TILE = 512


def _mla_flash_kernel(q_ref, k_ref, v_ref, o_ref, s_sc, p_sc, alpha_sc, m_sc, acc_sc):
    qi = pl.program_id(2)
    t = q_ref.shape[2]
    q = q_ref[0, 0]

    def scores(j):
        r0 = pl.multiple_of(j * t, t)
        return _dot_nt(q, k_ref[0, 0, pl.ds(r0, t), :])

    def softmax(s, slot):
        m_prev = m_sc[...]
        m_new = jnp.maximum(m_prev, jnp.max(s, axis=1, keepdims=True))
        alpha_sc[slot] = jnp.exp2(m_prev - m_new)
        p_sc[slot] = jnp.exp2(s - jnp.concatenate([m_new] * (t // 128), axis=1)).astype(BF16)
        m_sc[...] = m_new

    def accumulate(j, slot):
        r0 = pl.multiple_of(j * t, t)
        pv = _dot(p_sc[slot], v_ref[0, 0, pl.ds(r0, t), :])
        acc_sc[...] = jnp.concatenate([alpha_sc[slot]] * (MLA_VP // 128), axis=1) * acc_sc[...] + pv

    m_sc[...] = jnp.full(m_sc.shape, NEG, F32)
    acc_sc[...] = jnp.zeros(acc_sc.shape, F32)
    p_sc[1] = jnp.zeros(p_sc.shape[1:], BF16)
    alpha_sc[1] = jnp.ones(alpha_sc.shape[1:], F32)
    s_sc[0] = scores(0)

    def stage(j, slot):
        s_sc[1 - slot] = scores(j + 1)
        softmax(s_sc[slot], slot)
        accumulate(jnp.maximum(j - 1, 0), 1 - slot)

    def body(i, carry):
        stage(2 * i, 0)
        stage(2 * i + 1, 1)
        return carry

    lax.fori_loop(0, qi // 2, body, 0)

    def finish(slot):
        s = s_sc[slot]
        row = lax.broadcasted_iota(jnp.int32, s.shape, 0)
        col = lax.broadcasted_iota(jnp.int32, s.shape, 1)
        softmax(jnp.where(col <= row, s, NEG), slot)
        accumulate(jnp.maximum(qi - 1, 0), 1 - slot)
        accumulate(qi, slot)
        acc = acc_sc[...]
        o_ref[0] = (acc[:, :MLA_V_DIM] / acc[:, MLA_V_DIM:MLA_V_DIM + 1]).astype(o_ref.dtype)

    @pl.when(qi % 2 == 0)
    def _():
        finish(0)

    @pl.when(qi % 2 == 1)
    def _():
        stage(qi - 1, 0)
        finish(1)


def _mla_flash(q, k, v):
    B, H, S, _ = q.shape
    t = MLA_TILE
    return pl.pallas_call(
        _mla_flash_kernel,
        grid=(B, H, S // t),
        in_specs=[
            pl.BlockSpec((1, 1, t, MLA_QK_DIM), lambda b, h, i: (b, h, i, 0)),
            pl.BlockSpec((1, 1, S, MLA_QK_DIM), lambda b, h, i: (b, h, 0, 0)),
            pl.BlockSpec((1, 1, S, MLA_VP), lambda b, h, i: (b, h, 0, 0)),
        ],
        out_specs=pl.BlockSpec((1, t, MLA_V_DIM), lambda b, h, i: (b, i, h)),
        out_shape=jax.ShapeDtypeStruct((B, S, MLA_V_W), BF16),
        scratch_shapes=[
            pltpu.VMEM((2, t, t), F32),
            pltpu.VMEM((2, t, t), BF16),
            pltpu.VMEM((2, t, 128), F32),
            pltpu.VMEM((t, 128), F32),
            pltpu.VMEM((t, MLA_VP), F32),
        ],
        compiler_params=_params("parallel", "parallel", "arbitrary"),
        name="mla_flash",
    )(q, k, v)


def _mix_kernel(x_ref, g_ref, ya_in_ref, yb_in_ref, wga_ref, wgb_ref, wa_ref, wb_ref, wo_ref, o_ref, h_sc, acc_sc):
    n = pl.program_id(1)

    @pl.when(n == 0)
    def _():
        h_sc[...] = _rms(x_ref[...], g_ref[...]).astype(BF16)
        acc_sc[...] = jnp.zeros(acc_sc.shape, F32)

    h = h_sc[...]
    ga = _dot(h, wga_ref[...])
    gb = _dot(h, wgb_ref[...])
    ya = _dot(ya_in_ref[...], wa_ref[...])
    yb = _dot(yb_in_ref[...], wb_ref[...])
    mixed = jax.nn.sigmoid(ga) * ya + jax.nn.sigmoid(gb) * yb
    acc_sc[...] += _dot(mixed.astype(BF16), wo_ref[...])

    @pl.when(n == pl.num_programs(1) - 1)
    def _():
        o_ref[...] = x_ref[...] + acc_sc[...]


def _mix(x2, g, swa_out, mla_out, w_ga, w_gb, w_a, w_b, w_o, tm=512, tn=512):
    T, D = x2.shape
    return pl.pallas_call(
        _mix_kernel,
        grid=(T // tm, D // tn),
        in_specs=[
            pl.BlockSpec((tm, D), lambda i, n: (i, 0)),
            pl.BlockSpec((1, D), lambda i, n: (0, 0)),
            pl.BlockSpec((tm, SWA_Q_W), lambda i, n: (i, 0)),
            pl.BlockSpec((tm, MLA_V_W), lambda i, n: (i, 0)),
            pl.BlockSpec((D, tn), lambda i, n: (0, n)),
            pl.BlockSpec((D, tn), lambda i, n: (0, n)),
            pl.BlockSpec((SWA_Q_W, tn), lambda i, n: (0, n)),
            pl.BlockSpec((MLA_V_W, tn), lambda i, n: (0, n)),
            pl.BlockSpec((tn, D), lambda i, n: (n, 0)),
        ],
        out_specs=pl.BlockSpec((tm, D), lambda i, n: (i, 0)),
        out_shape=jax.ShapeDtypeStruct((T, D), F32),
        scratch_shapes=[pltpu.VMEM((tm, D), BF16), pltpu.VMEM((tm, D), F32)],
        compiler_params=_params("parallel", "arbitrary"),
        name="mix",
    )(x2, g, swa_out, mla_out, w_ga, w_gb, w_a, w_b, w_o)


PEER_LANES = 128
CAND_ROWS = 16 + 7 * 8 + 8


def _extract_top(s, iters, exact):
    n_rows = s.shape[0]
    rows = lax.broadcasted_iota(jnp.int32, s.shape, 0)
    rank = jnp.full(s.shape, float(iters), F32)
    vals = []
    for r in range(iters):
        m = jnp.max(s, axis=0, keepdims=True)
        if exact:
            idx = jnp.min(jnp.where(s == m, rows, n_rows), axis=0, keepdims=True)
            sel = rows == idx
        else:
            sel = s == m
        rank = jnp.where(sel, float(r), rank)
        s = jnp.where(sel, -jnp.inf, s)
        vals.append(m)
    return rank, vals


def _ranked_excess(rank, iters):
    return jnp.sum(jnp.where(rank < float(iters), 1.0, 0.0), axis=0, keepdims=True) - float(iters)


def _peer_route_kernel(x_ref, g_ref, wq_ref, keys_ref, h_out, r2_out, n1_out, e1_out, e2_out, q_sc):
    tm = x_ref.shape[0]
    h = _rms(x_ref[...], g_ref[...]).astype(BF16)
    h_out[...] = h
    q_sc[...] = _dot(h, wq_ref[...]).astype(BF16)

    def route(hd, t0, exact):
        c0 = pl.multiple_of(hd * 2 * PEER_HALF, 2 * PEER_HALF)
        q1 = q_sc[t0:t0 + PEER_LANES, pl.ds(c0, PEER_HALF)]
        q2 = q_sc[t0:t0 + PEER_LANES, pl.ds(c0 + PEER_HALF, PEER_HALF)]
        s1 = _dot_nt(keys_ref[2 * hd], q1)
        s2 = _dot_nt(keys_ref[2 * hd + 1], q2)
        r1, v1 = _extract_top(s1, PEER_TOPK, exact)
        r2, v2 = _extract_top(s2, PEER_TOPK, exact)
        v2_lo = jnp.concatenate(v2[:8], axis=0)
        v2_all = jnp.concatenate(v2, axis=0)
        slabs = [v1[0] + v2_all]
        for a in range(1, 8):
            slabs.append(v1[a] + v2_lo)
        slabs.append(jnp.concatenate(v1[8:], axis=0) + v2[0])
        cand = jnp.concatenate(slabs, axis=0)
        rc, vc = _extract_top(cand, PEER_TOPK, exact)
        picked = jnp.where(rc < float(PEER_TOPK), 1.0, 0.0)
        z = jnp.ones_like(vc[0])
        for r in range(1, PEER_TOPK):
            z = z + jnp.exp(vc[r] - vc[0])
        n1 = jnp.zeros_like(r1)
        n1 = jnp.where(r1 == 0.0, jnp.sum(picked[0:16], axis=0, keepdims=True), n1)
        for a in range(1, 8):
            n_a = jnp.sum(picked[8 + 8 * a:16 + 8 * a], axis=0, keepdims=True)
            n1 = jnp.where(r1 == float(a), n_a, n1)
        for a in range(8, 16):
            n1 = jnp.where(r1 == float(a), picked[64 + a:65 + a], n1)
        r2_out[hd, :, t0:t0 + PEER_LANES] = r2.astype(r2_out.dtype)
        n1_out[hd, :, t0:t0 + PEER_LANES] = n1
        e1_out[hd, :, t0:t0 + PEER_LANES] = jnp.exp(s1 - v1[0])
        e2_out[hd, :, t0:t0 + PEER_LANES] = (jnp.exp(s2 - v2[0]) / z).astype(e2_out.dtype)
        if exact:
            return None
        excess = (_ranked_excess(r1, PEER_TOPK) + _ranked_excess(r2, PEER_TOPK)
                  + _ranked_excess(rc, PEER_TOPK))
        return jnp.max(excess)

    def head_body(hd, carry):
        starts = [ch * PEER_LANES for ch in range(tm // PEER_LANES)]
        ties = [route(hd, t0, exact=False) for t0 in starts]

        @pl.when(functools.reduce(jnp.maximum, ties) > 0.0)
        def _():
            for t0 in starts:
                route(hd, t0, exact=True)
        return carry

    lax.fori_loop(0, PEER_HEADS, head_body, 0)


def _peer_route(x1, g, w_q, keys, tm=512):
    T, D = x1.shape
    side = jax.ShapeDtypeStruct((PEER_HEADS, PEER_N_KEYS, T), F32)
    side_bf = jax.ShapeDtypeStruct((PEER_HEADS, PEER_N_KEYS, T), BF16)
    side_spec = pl.BlockSpec((PEER_HEADS, PEER_N_KEYS, tm), lambda i: (0, 0, i))
    return pl.pallas_call(
        _peer_route_kernel,
        grid=(T // tm,),
        in_specs=[
            pl.BlockSpec((tm, D), lambda i: (i, 0)),
            pl.BlockSpec((1, D), lambda i: (0, 0)),
            pl.BlockSpec(w_q.shape, lambda i: (0, 0)),
            pl.BlockSpec(keys.shape, lambda i: (0, 0, 0)),
        ],
        out_specs=[pl.BlockSpec((tm, D), lambda i: (i, 0)), side_spec, side_spec, side_spec, side_spec],
        out_shape=[jax.ShapeDtypeStruct((T, D), BF16), side_bf, side, side, side_bf],
        scratch_shapes=[pltpu.VMEM((tm, w_q.shape[1]), BF16)],
        compiler_params=_params("parallel"),
        name="peer_route",
    )(x1, g, w_q, keys)


def _peer_dense_kernel(h_ref, u_ref, vt_ref, r2_ref, n1_ref, e1_ref, e2_ref, o_ref, acc_sc):
    e = pl.program_id(1)
    te = u_ref.shape[0]

    @pl.when(e == 0)
    def _():
        acc_sc[...] = jnp.zeros(acc_sc.shape, F32)

    a_t = _dot_nt(u_ref[...], h_ref[...])
    act = (0.5 * a_t * (1.0 + lax.erf(a_t * math.sqrt(0.5)))).astype(BF16)
    zero = jnp.zeros((), BF16)
    w_rows = []
    for ii in range(te // PEER_N_KEYS):
        i = e * (te // PEER_N_KEYS) + ii
        gate = None
        for hd in range(PEER_HEADS):
            n1 = n1_ref[hd, pl.ds(i, 1), :].astype(BF16)
            e1 = e1_ref[hd, pl.ds(i, 1), :].astype(BF16)
            term = jnp.where(r2_ref[hd] < n1, e2_ref[hd] * e1, zero)
            gate = term if gate is None else gate + term
        w_rows.append(act[ii * PEER_N_KEYS:(ii + 1) * PEER_N_KEYS] * gate)
    w = jnp.concatenate(w_rows, axis=0) if len(w_rows) > 1 else w_rows[0]
    acc_sc[...] += _dot(vt_ref[...], w)

    @pl.when(e == pl.num_programs(1) - 1)
    def _():
        o_ref[...] = acc_sc[...].T


def _peer_dense(h2, u_bf, vt_bf, r2, n1, e1, e2, tm=1024, te=512):
    T, D = h2.shape
    once = pl.Buffered(1)
    side_spec = pl.BlockSpec((PEER_HEADS, PEER_N_KEYS, tm), lambda i, e: (0, 0, i), pipeline_mode=once)
    return pl.pallas_call(
        _peer_dense_kernel,
        grid=(T // tm, PEER_N_EXPERTS // te),
        in_specs=[
            pl.BlockSpec((tm, D), lambda i, e: (i, 0), pipeline_mode=once),
            pl.BlockSpec((te, D), lambda i, e: (e, 0)),
            pl.BlockSpec((D, te), lambda i, e: (0, e)),
            side_spec, side_spec, side_spec, side_spec,
        ],
        out_specs=pl.BlockSpec((tm, D), lambda i, e: (i, 0), pipeline_mode=once),
        out_shape=jax.ShapeDtypeStruct((T, D), F32),
        scratch_shapes=[pltpu.VMEM((D, tm), F32)],
        compiler_params=_params("parallel", "arbitrary"),
        name="peer_dense",
    )(h2, u_bf, vt_bf, r2, n1, e1, e2)


def _final_kernel(x_ref, p_ref, g_ref, o_ref):
    o_ref[...] = _rms(x_ref[...] + p_ref[...], g_ref[...])


def _final_norm(x1, peer_out, g, tm=512):
    T, D = x1.shape
    return pl.pallas_call(
        _final_kernel,
        grid=(T // tm,),
        in_specs=[
            pl.BlockSpec((tm, D), lambda i: (i, 0)),
            pl.BlockSpec((tm, D), lambda i: (i, 0)),
            pl.BlockSpec((1, D), lambda i: (0, 0)),
        ],
        out_specs=pl.BlockSpec((tm, D), lambda i: (i, 0)),
        out_shape=jax.ShapeDtypeStruct((T, D), F32),
        compiler_params=_params("parallel"),
        name="final_norm",
    )(x1, peer_out, g)


def _layer(x2, pos_col, pos_row, batch, seq, g_mix, w_in, sinks, g_cq, w_uq, g_ckv, w_ukv, w_a_proj, w_b_proj,
           w_o, g_ffn, w_peer_q, peer_keys, peer_u, peer_v):
    D = x2.shape[1]
    o = np.cumsum([0, SWA_Q_W, SWA_KV_W, SWA_KV_W, MLA_Q_RANK, MLA_KV_RANK, MLA_ROPE_DIM, D, D])
    w_qa, w_ka, w_va, w_cq, w_ckv, w_kr, w_ga, w_gb = [w_in[:, o[j]:o[j + 1]] for j in range(8)]
    half = MLA_ROPE_DIM // 2
    w_kr_swapped = jnp.concatenate([-w_kr[:, half:], w_kr[:, :half]], axis=1)
    w_qa = w_qa * (SWA_HEAD_DIM ** -0.5 * LOG2E)
    w_attn = jnp.concatenate([w_qa, w_cq, w_ka, w_va, w_ckv, w_kr, w_kr_swapped], axis=1).astype(BF16)

    q_h, c_q, k_h, v_h, c_kv, k_r2 = _attn_inproj(x2, g_mix.reshape(1, D), w_attn)
    swa_out = _swa(q_h, k_h, v_h, pos_col, pos_row, sinks.astype(F32), seq)

    w_uq3 = w_uq.reshape(MLA_Q_RANK, MLA_HEADS, MLA_QK_DIM)
    wq_n = w_uq3[:, :, :MLA_NOPE_DIM].reshape(MLA_Q_RANK, MLA_HEADS * MLA_NOPE_DIM).astype(BF16)
    t1 = w_uq3[:, :, MLA_NOPE_DIM:MLA_NOPE_DIM + half]
    t2 = w_uq3[:, :, MLA_NOPE_DIM + half:]
    wq_r = jnp.concatenate([t1, t2], axis=2).reshape(MLA_Q_RANK, MLA_HEADS * MLA_ROPE_DIM).astype(BF16)
    wq_s = jnp.concatenate([-t2, t1], axis=2).reshape(MLA_Q_RANK, MLA_HEADS * MLA_ROPE_DIM).astype(BF16)
    inv = 1.0 / (ROPE_THETA ** (jnp.arange(0, MLA_ROPE_DIM, 2, dtype=F32) / MLA_ROPE_DIM))
    inv128 = jnp.tile(inv, 128 // half).reshape(1, 128)
    q, k, v = _mla_prep(c_q, c_kv, k_r2, pos_col, inv128, g_cq.reshape(1, -1), g_ckv.reshape(1, -1),
                        wq_n, wq_r, wq_s, w_ukv.astype(BF16), batch, seq)
    mla_out = _mla_flash(q, k, v).reshape(batch * seq, MLA_V_W)

    x1 = _mix(x2, g_mix.reshape(1, D), swa_out, mla_out, w_ga.astype(BF16), w_gb.astype(BF16),
              w_a_proj.astype(BF16), w_b_proj.astype(BF16), w_o.astype(BF16))

    keys = peer_keys.reshape(PEER_HEADS * 2, PEER_N_KEYS, PEER_HALF).astype(BF16)
    h2, r2, n1, e1, e2 = _peer_route(x1, g_ffn.reshape(1, D), w_peer_q.astype(BF16), keys)
    peer_out = _peer_dense(h2, peer_u.astype(BF16), peer_v.T.astype(BF16), r2, n1, e1, e2)
    return x1, peer_out


def kernel(x, positions, g_mix, w_in, sinks, g_cq, w_uq, g_ckv, w_ukv, w_a_proj, w_b_proj, w_o, g_ffn, w_peer_q,
           peer_keys, peer_u, peer_v, g_final):
    batch, seq, D = x.shape
    depth = g_mix.shape[0]
    assert seq % 1024 == 0 and D % 512 == 0, (seq, D)
    x2 = x.reshape(batch * seq, D)
    pos_f = positions.astype(F32)
    pos_col = pos_f.reshape(batch * seq, 1)
    pos_row = pos_f.reshape(1, batch * seq)
    peer_out = None
    for l in range(depth):
        if peer_out is not None:
            x2 = x2 + peer_out
        x2, peer_out = _layer(x2, pos_col, pos_row, batch, seq, g_mix[l], w_in[l], sinks[l], g_cq[l], w_uq[l],
                              g_ckv[l], w_ukv[l], w_a_proj[l], w_b_proj[l], w_o[l], g_ffn[l], w_peer_q[l],
                              peer_keys[l], peer_u[l], peer_v[l])
    out = _final_norm(x2, peer_out, g_final.reshape(1, D))
    return out.reshape(batch, seq, D)
```

```python
import functools
import math

import numpy as np
import jax
import jax.numpy as jnp
from jax import lax
from jax.experimental import pallas as pl
from jax.experimental.pallas import tpu as pltpu

F32 = jnp.float32
BF16 = jnp.bfloat16

EPS = 1e-6
NEG = -1e30
BLOCK = 128
SWA_HEADS = 16
SWA_KV_HEADS = 4
SWA_GROUP = SWA_HEADS // SWA_KV_HEADS
SWA_HEAD_DIM = 64
SWA_WINDOW = 128
SWA_Q_W = SWA_HEADS * SWA_HEAD_DIM
SWA_KV_W = SWA_KV_HEADS * SWA_HEAD_DIM
MLA_HEADS = 8
MLA_Q_RANK = 512
MLA_KV_RANK = 256
MLA_NOPE_DIM = 128
MLA_ROPE_DIM = 64
MLA_QK_DIM = MLA_NOPE_DIM + MLA_ROPE_DIM
MLA_V_DIM = 128
MLA_V_W = MLA_HEADS * MLA_V_DIM
ROPE_THETA = 10000.0
PEER_HEADS = 8
PEER_N_KEYS = 128
PEER_N_EXPERTS = PEER_N_KEYS * PEER_N_KEYS
PEER_HALF = 128
PEER_TOPK = 16

V7X_VMEM_BYTES = 64 * 1024 * 1024
VMEM_LIMIT = V7X_VMEM_BYTES - 8 * 1024 * 1024


def _params(*semantics):
    return pltpu.CompilerParams(dimension_semantics=semantics, vmem_limit_bytes=VMEM_LIMIT)


def _rms(xf, g):
    y = xf * lax.rsqrt(jnp.mean(xf * xf, axis=-1, keepdims=True) + EPS)
    return y * g


def _dot(a, b):
    return jnp.dot(a, b, preferred_element_type=F32)


def _dot_nt(a, b):
    return lax.dot_general(a, b, (((1,), (1,)), ((), ())), preferred_element_type=F32)


ATTN_SPLITS = (SWA_Q_W, MLA_Q_RANK, SWA_KV_W, SWA_KV_W, MLA_KV_RANK, 2 * MLA_ROPE_DIM)
ATTN_W = sum(ATTN_SPLITS)
LOG2E = math.log2(math.e)


def _attn_inproj_kernel(x_ref, g_ref, w_ref, q_out, cq_out, k_out, v_out, ckv_out, kr_out):
    h = _rms(x_ref[...], g_ref[...]).astype(BF16)
    z = _dot(h, w_ref[...]).astype(BF16)
    d = SWA_HEAD_DIM
    off = 0
    for hd in range(SWA_HEADS):
        q_out[hd] = z[:, off + hd * d:off + (hd + 1) * d]
    off += SWA_Q_W
    cq_out[...] = z[:, off:off + MLA_Q_RANK]
    off += MLA_Q_RANK
    for hd in range(SWA_KV_HEADS):
        k_out[hd] = z[:, off + hd * d:off + (hd + 1) * d]
    off += SWA_KV_W
    for hd in range(SWA_KV_HEADS):
        v_out[hd] = z[:, off + hd * d:off + (hd + 1) * d]
    off += SWA_KV_W
    ckv_out[...] = z[:, off:off + MLA_KV_RANK]
    off += MLA_KV_RANK
    kr_out[...] = z[:, off:off + 2 * MLA_ROPE_DIM]


def _attn_inproj(x2, g, w_attn, tm=512):
    T, D = x2.shape
    d = SWA_HEAD_DIM
    row = lambda i: (i, 0)
    head_major = lambda i: (0, i, 0)
    return pl.pallas_call(
        _attn_inproj_kernel,
        grid=(T // tm,),
        in_specs=[
            pl.BlockSpec((tm, D), row),
            pl.BlockSpec((1, D), lambda i: (0, 0)),
            pl.BlockSpec((D, ATTN_W), lambda i: (0, 0)),
        ],
        out_specs=[
            pl.BlockSpec((SWA_HEADS, tm, d), head_major),
            pl.BlockSpec((tm, MLA_Q_RANK), row),
            pl.BlockSpec((SWA_KV_HEADS, tm, d), head_major),
            pl.BlockSpec((SWA_KV_HEADS, tm, d), head_major),
            pl.BlockSpec((tm, MLA_KV_RANK), row),
            pl.BlockSpec((tm, 2 * MLA_ROPE_DIM), row),
        ],
        out_shape=[
            jax.ShapeDtypeStruct((SWA_HEADS, T, d), BF16),
            jax.ShapeDtypeStruct((T, MLA_Q_RANK), BF16),
            jax.ShapeDtypeStruct((SWA_KV_HEADS, T, d), BF16),
            jax.ShapeDtypeStruct((SWA_KV_HEADS, T, d), BF16),
            jax.ShapeDtypeStruct((T, MLA_KV_RANK), BF16),
            jax.ShapeDtypeStruct((T, 2 * MLA_ROPE_DIM), BF16),
        ],
        compiler_params=_params("parallel"),
        name="attn_inproj",
    )(x2, g, w_attn)


SWA_TILE = 512
SWA_SUB = SWA_TILE // BLOCK


def _swa_kernel(tiles_per_seq, sinks_ref, q_ref, kc_ref, kp_ref, vc_ref, vp_ref, pc_ref, prc_ref, prp_ref,
                o_ref, kbuf, vbuf, pkbuf):
    i = pl.program_id(0)
    kbuf[:, 0:BLOCK, :] = kp_ref[...]
    kbuf[:, BLOCK:, :] = kc_ref[...]
    vbuf[:, 0:BLOCK, :] = vp_ref[...]
    vbuf[:, BLOCK:, :] = vc_ref[...]
    pkbuf[:, 0:BLOCK] = prp_ref[...]
    pkbuf[:, BLOCK:] = prc_ref[...]
    first_tile = (i % tiles_per_seq) == 0

    qi = lax.broadcasted_iota(jnp.int32, (BLOCK, 2 * BLOCK), 0)
    kj = lax.broadcasted_iota(jnp.int32, (BLOCK, 2 * BLOCK), 1)
    rel = BLOCK + qi - kj
    band = (rel >= 0) & (rel < SWA_WINDOW)
    d = SWA_HEAD_DIM

    for c in range(SWA_SUB):
        r0 = c * BLOCK
        pq = pc_ref[r0:r0 + BLOCK, :]
        pk = pkbuf[:, r0:r0 + 2 * BLOCK]
        dist = jnp.abs(pq - pk) * LOG2E
        if c == 0:
            mask = band & (jnp.logical_not(first_tile) | (kj >= BLOCK))
        else:
            mask = band
        for g in range(SWA_KV_HEADS):
            qs = jnp.concatenate([q_ref[g * SWA_GROUP + hh, r0:r0 + BLOCK, :] for hh in range(SWA_GROUP)], axis=0)
            logits = _dot_nt(qs, kbuf[g, r0:r0 + 2 * BLOCK, :])
            probs, denoms = [], []
            for hh in range(SWA_GROUP):
                h = g * SWA_GROUP + hh
                slope = 2.0 ** (-8.0 * (h + 1) / SWA_HEADS)
                lh = logits[hh * BLOCK:(hh + 1) * BLOCK] - slope * dist
                lh = jnp.where(mask, lh, NEG)
                sink = sinks_ref[h] * LOG2E
                m = jnp.maximum(jnp.max(lh, axis=-1, keepdims=True), sink)
                p = jnp.exp2(lh - m)
                denoms.append(jnp.sum(p, axis=-1, keepdims=True) + jnp.exp2(sink - m))
                probs.append(p.astype(BF16))
            o = _dot(jnp.concatenate(probs, axis=0), vbuf[g, r0:r0 + 2 * BLOCK, :])
            for hh in range(SWA_GROUP):
                h = g * SWA_GROUP + hh
                o_ref[r0:r0 + BLOCK, h * d:(h + 1) * d] = (o[hh * BLOCK:(hh + 1) * BLOCK] / denoms[hh]).astype(
                    o_ref.dtype)


def _swa(q_h, k_h, v_h, pos_col, pos_row, sinks, seq):
    T = q_h.shape[1]
    d = SWA_HEAD_DIM
    tiles_per_seq = seq // SWA_TILE

    def prev_blk(i):
        return jnp.where(i % tiles_per_seq == 0, i * SWA_SUB, i * SWA_SUB - 1)

    cur = lambda i: (0, i, 0)
    prev = lambda i: (0, prev_blk(i), 0)
    return pl.pallas_call(
        functools.partial(_swa_kernel, tiles_per_seq),
        grid=(T // SWA_TILE,),
        in_specs=[
            pl.BlockSpec(memory_space=pltpu.SMEM),
            pl.BlockSpec((SWA_HEADS, SWA_TILE, d), cur),
            pl.BlockSpec((SWA_KV_HEADS, SWA_TILE, d), cur),
            pl.BlockSpec((SWA_KV_HEADS, BLOCK, d), prev),
            pl.BlockSpec((SWA_KV_HEADS, SWA_TILE, d), cur),
            pl.BlockSpec((SWA_KV_HEADS, BLOCK, d), prev),
            pl.BlockSpec((SWA_TILE, 1), lambda i: (i, 0)),
            pl.BlockSpec((1, SWA_TILE), lambda i: (0, i)),
            pl.BlockSpec((1, BLOCK), lambda i: (0, prev_blk(i))),
        ],
        out_specs=pl.BlockSpec((SWA_TILE, SWA_Q_W), lambda i: (i, 0)),
        out_shape=jax.ShapeDtypeStruct((T, SWA_Q_W), BF16),
        scratch_shapes=[
            pltpu.VMEM((SWA_KV_HEADS, SWA_TILE + BLOCK, d), BF16),
            pltpu.VMEM((SWA_KV_HEADS, SWA_TILE + BLOCK, d), BF16),
            pltpu.VMEM((1, SWA_TILE + BLOCK), F32),
        ],
        compiler_params=_params("parallel"),
        name="swa",
    )(sinks, q_h, k_h, k_h, v_h, v_h, pos_col, pos_row, pos_row)


MLA_VP = 2 * MLA_V_DIM


def _mla_prep_kernel(cq_ref, ckv_ref, kr_ref, pos_ref, inv_ref, gcq_ref, gckv_ref, wqn_ref, wqr_ref, wqs_ref,
                     wkv_ref, q_out, k_out, v_out):
    scale = MLA_QK_DIM ** -0.5 * math.log2(math.e)
    ang = pos_ref[...] * inv_ref[...]
    cos = jnp.cos(ang)
    sin = jnp.sin(ang)
    cos4 = jnp.concatenate([cos] * 4, axis=1)
    sin4 = jnp.concatenate([sin] * 4, axis=1)

    cn = _rms(cq_ref[...].astype(F32), gcq_ref[...]).astype(BF16)
    qn = _dot(cn, wqn_ref[...])
    qr = _dot(cn, wqr_ref[...])
    qs = _dot(cn, wqs_ref[...])
    qrot = qr * cos4 + qs * sin4

    kvn = _rms(ckv_ref[...].astype(F32), gckv_ref[...]).astype(BF16)
    kv = _dot(kvn, wkv_ref[...])
    kr = kr_ref[...].astype(F32)
    krot = (kr[:, :MLA_ROPE_DIM] * cos[:, :MLA_ROPE_DIM]
            + kr[:, MLA_ROPE_DIM:] * sin[:, :MLA_ROPE_DIM]).astype(BF16)

    lane = lax.broadcasted_iota(jnp.int32, (kr.shape[0], MLA_VP - MLA_V_DIM), 1)
    ones_col = jnp.where(lane == 0, 1.0, 0.0).astype(BF16)
    for h in range(MLA_HEADS):
        q_h = jnp.concatenate(
            [qn[:, h * MLA_NOPE_DIM:(h + 1) * MLA_NOPE_DIM], qrot[:, h * MLA_ROPE_DIM:(h + 1) * MLA_ROPE_DIM]],
            axis=1) * scale
        q_out[0, h] = q_h.astype(BF16)
        kn_h = kv[:, h * 256:h * 256 + MLA_NOPE_DIM].astype(BF16)
        k_out[0, h] = jnp.concatenate([kn_h, krot], axis=1)
        v_out[0, h, :, :MLA_V_DIM] = kv[:, h * 256 + MLA_NOPE_DIM:(h + 1) * 256].astype(BF16)
        v_out[0, h, :, MLA_V_DIM:] = ones_col


def _mla_prep(c_q, c_kv, k_r2, pos_col, inv128, g_cq, g_ckv, wq_n, wq_r, wq_s, w_kv, batch, seq, tm=512):
    T = c_q.shape[0]
    tps = seq // tm
    const = lambda i: (0, 0)
    out_idx = lambda i: (i // tps, 0, i % tps, 0)
    return pl.pallas_call(
        _mla_prep_kernel,
        grid=(T // tm,),
        in_specs=[
            pl.BlockSpec((tm, MLA_Q_RANK), lambda i: (i, 0)),
            pl.BlockSpec((tm, MLA_KV_RANK), lambda i: (i, 0)),
            pl.BlockSpec((tm, 2 * MLA_ROPE_DIM), lambda i: (i, 0)),
            pl.BlockSpec((tm, 1), lambda i: (i, 0)),
            pl.BlockSpec((1, 128), const),
            pl.BlockSpec((1, MLA_Q_RANK), const),
            pl.BlockSpec((1, MLA_KV_RANK), const),
            pl.BlockSpec(wq_n.shape, const),
            pl.BlockSpec(wq_r.shape, const),
            pl.BlockSpec(wq_s.shape, const),
            pl.BlockSpec(w_kv.shape, const),
        ],
        out_specs=[
            pl.BlockSpec((1, MLA_HEADS, tm, MLA_QK_DIM), out_idx),
            pl.BlockSpec((1, MLA_HEADS, tm, MLA_QK_DIM), out_idx),
            pl.BlockSpec((1, MLA_HEADS, tm, MLA_VP), out_idx),
        ],
        out_shape=[
            jax.ShapeDtypeStruct((batch, MLA_HEADS, seq, MLA_QK_DIM), BF16),
            jax.ShapeDtypeStruct((batch, MLA_HEADS, seq, MLA_QK_DIM), BF16),
            jax.ShapeDtypeStruct((batch, MLA_HEADS, seq, MLA_VP), BF16),
        ],
        compiler_params=_params("parallel"),
        name="mla_prep",
    )(c_q, c_kv, k_r2, pos_col, inv128, g_cq, g_ckv, wq_n, wq_r, wq_s, w_kv)


MLA_TILE = 512
MLA_HP = 2


def _mla_flash_kernel(q_ref, k_ref, v_ref, o_ref, s_sc, p_sc, alpha_sc, m_sc, acc_sc):
    qi = pl.program_id(2)
    t = q_ref.shape[2]
    heads = range(MLA_HP)

    def scores(hh, j):
        r0 = pl.multiple_of(j * t, t)
        return _dot_nt(q_ref[0, hh], k_ref[0, hh, pl.ds(r0, t), :])

    def softmax(hh, s, slot):
        m_prev = m_sc[hh]
        m_new = jnp.maximum(m_prev, jnp.max(s, axis=1, keepdims=True))
        alpha_sc[slot, hh] = jnp.exp2(m_prev - m_new)
        p_sc[slot, hh] = jnp.exp2(s - jnp.concatenate([m_new] * (t // 128), axis=1)).astype(BF16)
        m_sc[hh] = m_new

    def accumulate(hh, j, slot):
        r0 = pl.multiple_of(j * t, t)
        pv = _dot(p_sc[slot, hh], v_ref[0, hh, pl.ds(r0, t), :])
        acc_sc[hh] = jnp.concatenate([alpha_sc[slot, hh]] * (MLA_VP // 128), axis=1) * acc_sc[hh] + pv

    m_sc[...] = jnp.full(m_sc.shape, NEG, F32)
    acc_sc[...] = jnp.zeros(acc_sc.shape, F32)
    p_sc[1] = jnp.zeros(p_sc.shape[1:], BF16)
    alpha_sc[1] = jnp.ones(alpha_sc.shape[1:], F32)
    for hh in heads:
        s_sc[0, hh] = scores(hh, 0)

    def stage(j, slot):
        for hh in heads:
            s_sc[1 - slot, hh] = scores(hh, j + 1)
            softmax(hh, s_sc[slot, hh], slot)
            accumulate(hh, jnp.maximum(j - 1, 0), 1 - slot)

    def body(i, carry):
        stage(2 * i, 0)
        stage(2 * i + 1, 1)
        return carry

    lax.fori_loop(0, qi // 2, body, 0)

    def finish(slot):
        row = lax.broadcasted_iota(jnp.int32, (t, t), 0)
        col = lax.broadcasted_iota(jnp.int32, (t, t), 1)
        for hh in heads:
            softmax(hh, jnp.where(col <= row, s_sc[slot, hh], NEG), slot)
            accumulate(hh, jnp.maximum(qi - 1, 0), 1 - slot)
        for hh in heads:
            accumulate(hh, qi, slot)
            acc = acc_sc[hh]
            o_ref[0, :, hh * MLA_V_DIM:(hh + 1) * MLA_V_DIM] = (
                acc[:, :MLA_V_DIM] / acc[:, MLA_V_DIM:MLA_V_DIM + 1]).astype(o_ref.dtype)

    @pl.when(qi % 2 == 0)
    def _():
        finish(0)

    @pl.when(qi % 2 == 1)
    def _():
        stage(qi - 1, 0)
        finish(1)


def _mla_flash(q, k, v):
    B, H, S, _ = q.shape
    t = MLA_TILE
    hp = MLA_HP
    return pl.pallas_call(
        _mla_flash_kernel,
        grid=(B, H // hp, S // t),
        in_specs=[
            pl.BlockSpec((1, hp, t, MLA_QK_DIM), lambda b, h, i: (b, h, i, 0)),
            pl.BlockSpec((1, hp, S, MLA_QK_DIM), lambda b, h, i: (b, h, 0, 0)),
            pl.BlockSpec((1, hp, S, MLA_VP), lambda b, h, i: (b, h, 0, 0)),
        ],
        out_specs=pl.BlockSpec((1, t, hp * MLA_V_DIM), lambda b, h, i: (b, i, h)),
        out_shape=jax.ShapeDtypeStruct((B, S, MLA_V_W), BF16),
        scratch_shapes=[
            pltpu.VMEM((2, hp, t, t), F32),
            pltpu.VMEM((2, hp, t, t), BF16),
            pltpu.VMEM((2, hp, t, 128), F32),
            pltpu.VMEM((hp, t, 128), F32),
            pltpu.VMEM((hp, t, MLA_VP), F32),
        ],
        compiler_params=_params("parallel", "parallel", "arbitrary"),
        name="mla_flash",
    )(q, k, v)


def _mix_kernel(x_ref, g_ref, ya_in_ref, yb_in_ref, wga_ref, wgb_ref, wa_ref, wb_ref, wo_ref, o_ref, h_sc, acc_sc):
    n = pl.program_id(1)

    @pl.when(n == 0)
    def _():
        h_sc[...] = _rms(x_ref[...], g_ref[...]).astype(BF16)
        acc_sc[...] = jnp.zeros(acc_sc.shape, F32)

    h = h_sc[...]
    ga = _dot(h, wga_ref[...])
    gb = _dot(h, wgb_ref[...])
    ya = _dot(ya_in_ref[...], wa_ref[...])
    yb = _dot(yb_in_ref[...], wb_ref[...])
    mixed = jax.nn.sigmoid(ga) * ya + jax.nn.sigmoid(gb) * yb
    acc_sc[...] += _dot(mixed.astype(BF16), wo_ref[...])

    @pl.when(n == pl.num_programs(1) - 1)
    def _():
        o_ref[...] = x_ref[...] + acc_sc[...]


def _mix(x2, g, swa_out, mla_out, w_ga, w_gb, w_a, w_b, w_o, tm=512, tn=512):
    T, D = x2.shape
    return pl.pallas_call(
        _mix_kernel,
        grid=(T // tm, D // tn),
        in_specs=[
            pl.BlockSpec((tm, D), lambda i, n: (i, 0)),
            pl.BlockSpec((1, D), lambda i, n: (0, 0)),
            pl.BlockSpec((tm, SWA_Q_W), lambda i, n: (i, 0)),
            pl.BlockSpec((tm, MLA_V_W), lambda i, n: (i, 0)),
            pl.BlockSpec((D, tn), lambda i, n: (0, n)),
            pl.BlockSpec((D, tn), lambda i, n: (0, n)),
            pl.BlockSpec((SWA_Q_W, tn), lambda i, n: (0, n)),
            pl.BlockSpec((MLA_V_W, tn), lambda i, n: (0, n)),
            pl.BlockSpec((tn, D), lambda i, n: (n, 0)),
        ],
        out_specs=pl.BlockSpec((tm, D), lambda i, n: (i, 0)),
        out_shape=jax.ShapeDtypeStruct((T, D), F32),
        scratch_shapes=[pltpu.VMEM((tm, D), BF16), pltpu.VMEM((tm, D), F32)],
        compiler_params=_params("parallel", "arbitrary"),
        name="mix",
    )(x2, g, swa_out, mla_out, w_ga, w_gb, w_a, w_b, w_o)


PEER_LANES = 128
CAND_ROWS = 16 + 7 * 8 + 8


def _extract_top(s, iters, exact):
    n_rows = s.shape[0]
    rows = lax.broadcasted_iota(jnp.int32, s.shape, 0)
    rank = jnp.full(s.shape, float(iters), F32)
    vals = []
    for r in range(iters):
        m = jnp.max(s, axis=0, keepdims=True)
        if exact:
            idx = jnp.min(jnp.where(s == m, rows, n_rows), axis=0, keepdims=True)
            sel = rows == idx
        else:
            sel = s == m
        rank = jnp.where(sel, float(r), rank)
        s = jnp.where(sel, -jnp.inf, s)
        vals.append(m)
    return rank, vals


def _ranked_excess(rank, iters):
    return jnp.sum(jnp.where(rank < float(iters), 1.0, 0.0), axis=0, keepdims=True) - float(iters)


def _peer_route_kernel(x_ref, g_ref, wq_ref, keys_ref, h_out, r2_out, n1_out, e1_out, e2_out, q_sc):
    tm = x_ref.shape[0]
    h = _rms(x_ref[...], g_ref[...]).astype(BF16)
    h_out[...] = h
    q_sc[...] = _dot(h, wq_ref[...]).astype(BF16)

    def route(hd, t0, exact):
        c0 = pl.multiple_of(hd * 2 * PEER_HALF, 2 * PEER_HALF)
        q1 = q_sc[t0:t0 + PEER_LANES, pl.ds(c0, PEER_HALF)]
        q2 = q_sc[t0:t0 + PEER_LANES, pl.ds(c0 + PEER_HALF, PEER_HALF)]
        s1 = _dot_nt(keys_ref[2 * hd], q1)
        s2 = _dot_nt(keys_ref[2 * hd + 1], q2)
        r1, v1 = _extract_top(s1, PEER_TOPK, exact)
        r2, v2 = _extract_top(s2, PEER_TOPK, exact)
        v2_lo = jnp.concatenate(v2[:8], axis=0)
        v2_all = jnp.concatenate(v2, axis=0)
        slabs = [v1[0] + v2_all]
        for a in range(1, 8):
            slabs.append(v1[a] + v2_lo)
        slabs.append(jnp.concatenate(v1[8:], axis=0) + v2[0])
        cand = jnp.concatenate(slabs, axis=0)
        rc, vc = _extract_top(cand, PEER_TOPK, exact)
        picked = jnp.where(rc < float(PEER_TOPK), 1.0, 0.0)
        z = jnp.ones_like(vc[0])
        for r in range(1, PEER_TOPK):
            z = z + jnp.exp(vc[r] - vc[0])
        cnt_lo = picked[0:8]
        for a in range(1, 8):
            cnt_lo = cnt_lo + picked[8 + 8 * a:16 + 8 * a]
        first_col = jnp.sum(picked[72:80], axis=0, keepdims=True)
        cnt_hi = jnp.sum(picked[8:16], axis=0, keepdims=True)
        n1 = jnp.where(r1 == 0.0, cnt_hi, 0.0)
        for b in range(8):
            cnt_b = cnt_lo[b:b + 1] + first_col if b == 0 else cnt_lo[b:b + 1]
            n1 = n1 + jnp.where(r1 < cnt_b, 1.0, 0.0)
        r2_out[hd, :, t0:t0 + PEER_LANES] = r2.astype(r2_out.dtype)
        n1_out[hd, :, t0:t0 + PEER_LANES] = n1
        e1_out[hd, :, t0:t0 + PEER_LANES] = jnp.exp(s1 - v1[0])
        e2_out[hd, :, t0:t0 + PEER_LANES] = (jnp.exp(s2 - v2[0]) / z).astype(e2_out.dtype)
        if exact:
            return None
        excess = (_ranked_excess(r1, PEER_TOPK) + _ranked_excess(r2, PEER_TOPK)
                  + _ranked_excess(rc, PEER_TOPK))
        return jnp.max(excess)

    def head_body(hd, carry):
        starts = [ch * PEER_LANES for ch in range(tm // PEER_LANES)]
        ties = [route(hd, t0, exact=False) for t0 in starts]

        @pl.when(functools.reduce(jnp.maximum, ties) > 0.0)
        def _():
            for t0 in starts:
                route(hd, t0, exact=True)
        return carry

    lax.fori_loop(0, PEER_HEADS, head_body, 0)


def _peer_route(x1, g, w_q, keys, tm=512):
    T, D = x1.shape
    side = jax.ShapeDtypeStruct((PEER_HEADS, PEER_N_KEYS, T), F32)
    side_bf = jax.ShapeDtypeStruct((PEER_HEADS, PEER_N_KEYS, T), BF16)
    side_spec = pl.BlockSpec((PEER_HEADS, PEER_N_KEYS, tm), lambda i: (0, 0, i))
    return pl.pallas_call(
        _peer_route_kernel,
        grid=(T // tm,),
        in_specs=[
            pl.BlockSpec((tm, D), lambda i: (i, 0)),
            pl.BlockSpec((1, D), lambda i: (0, 0)),
            pl.BlockSpec(w_q.shape, lambda i: (0, 0)),
            pl.BlockSpec(keys.shape, lambda i: (0, 0, 0)),
        ],
        out_specs=[pl.BlockSpec((tm, D), lambda i: (i, 0)), side_spec, side_spec, side_spec, side_spec],
        out_shape=[jax.ShapeDtypeStruct((T, D), BF16), side_bf, side, side, side_bf],
        scratch_shapes=[pltpu.VMEM((tm, w_q.shape[1]), BF16)],
        compiler_params=_params("parallel"),
        name="peer_route",
    )(x1, g, w_q, keys)


def _peer_dense_kernel(h_ref, u_ref, vt_ref, r2_ref, n1_ref, e1_ref, e2_ref, o_ref, acc_sc):
    e = pl.program_id(1)
    te = u_ref.shape[0]

    @pl.when(e == 0)
    def _():
        acc_sc[...] = jnp.zeros(acc_sc.shape, F32)

    a_t = _dot_nt(u_ref[...], h_ref[...])
    act = (0.5 * a_t * (1.0 + lax.erf(a_t * math.sqrt(0.5)))).astype(BF16)
    zero = jnp.zeros((), BF16)
    w_rows = []
    for ii in range(te // PEER_N_KEYS):
        i = e * (te // PEER_N_KEYS) + ii
        gate = None
        for hd in range(PEER_HEADS):
            n1 = n1_ref[hd, pl.ds(i, 1), :].astype(BF16)
            e1 = e1_ref[hd, pl.ds(i, 1), :].astype(BF16)
            term = jnp.where(r2_ref[hd] < n1, e2_ref[hd] * e1, zero)
            gate = term if gate is None else gate + term
        w_rows.append(act[ii * PEER_N_KEYS:(ii + 1) * PEER_N_KEYS] * gate)
    w = jnp.concatenate(w_rows, axis=0) if len(w_rows) > 1 else w_rows[0]
    acc_sc[...] += _dot(vt_ref[...], w)

    @pl.when(e == pl.num_programs(1) - 1)
    def _():
        o_ref[...] = acc_sc[...].T


def _peer_dense(h2, u_bf, vt_bf, r2, n1, e1, e2, tm=1024, te=512):
    T, D = h2.shape
    once = pl.Buffered(1)
    side_spec = pl.BlockSpec((PEER_HEADS, PEER_N_KEYS, tm), lambda i, e: (0, 0, i), pipeline_mode=once)
    return pl.pallas_call(
        _peer_dense_kernel,
        grid=(T // tm, PEER_N_EXPERTS // te),
        in_specs=[
            pl.BlockSpec((tm, D), lambda i, e: (i, 0), pipeline_mode=once),
            pl.BlockSpec((te, D), lambda i, e: (e, 0)),
            pl.BlockSpec((D, te), lambda i, e: (0, e)),
            side_spec, side_spec, side_spec, side_spec,
        ],
        out_specs=pl.BlockSpec((tm, D), lambda i, e: (i, 0), pipeline_mode=once),
        out_shape=jax.ShapeDtypeStruct((T, D), F32),
        scratch_shapes=[pltpu.VMEM((D, tm), F32)],
        compiler_params=_params("parallel", "arbitrary"),
        name="peer_dense",
    )(h2, u_bf, vt_bf, r2, n1, e1, e2)


def _final_kernel(x_ref, p_ref, g_ref, o_ref):
    o_ref[...] = _rms(x_ref[...] + p_ref[...], g_ref[...])


def _final_norm(x1, peer_out, g, tm=512):
    T, D = x1.shape
    return pl.pallas_call(
        _final_kernel,
        grid=(T // tm,),
        in_specs=[
            pl.BlockSpec((tm, D), lambda i: (i, 0)),
            pl.BlockSpec((tm, D), lambda i: (i, 0)),
            pl.BlockSpec((1, D), lambda i: (0, 0)),
        ],
        out_specs=pl.BlockSpec((tm, D), lambda i: (i, 0)),
        out_shape=jax.ShapeDtypeStruct((T, D), F32),
        compiler_params=_params("parallel"),
        name="final_norm",
    )(x1, peer_out, g)


def _layer(x2, pos_col, pos_row, batch, seq, g_mix, w_in, sinks, g_cq, w_uq, g_ckv, w_ukv, w_a_proj, w_b_proj,
           w_o, g_ffn, w_peer_q, peer_keys, peer_u, peer_v):
    D = x2.shape[1]
    o = np.cumsum([0, SWA_Q_W, SWA_KV_W, SWA_KV_W, MLA_Q_RANK, MLA_KV_RANK, MLA_ROPE_DIM, D, D])
    w_qa, w_ka, w_va, w_cq, w_ckv, w_kr, w_ga, w_gb = [w_in[:, o[j]:o[j + 1]] for j in range(8)]
    half = MLA_ROPE_DIM // 2
    w_kr_swapped = jnp.concatenate([-w_kr[:, half:], w_kr[:, :half]], axis=1)
    w_qa = w_qa * (SWA_HEAD_DIM ** -0.5 * LOG2E)
    w_attn = jnp.concatenate([w_qa, w_cq, w_ka, w_va, w_ckv, w_kr, w_kr_swapped], axis=1).astype(BF16)

    q_h, c_q, k_h, v_h, c_kv, k_r2 = _attn_inproj(x2, g_mix.reshape(1, D), w_attn)
    swa_out = _swa(q_h, k_h, v_h, pos_col, pos_row, sinks.astype(F32), seq)

    w_uq3 = w_uq.reshape(MLA_Q_RANK, MLA_HEADS, MLA_QK_DIM)
    wq_n = w_uq3[:, :, :MLA_NOPE_DIM].reshape(MLA_Q_RANK, MLA_HEADS * MLA_NOPE_DIM).astype(BF16)
    t1 = w_uq3[:, :, MLA_NOPE_DIM:MLA_NOPE_DIM + half]
    t2 = w_uq3[:, :, MLA_NOPE_DIM + half:]
    wq_r = jnp.concatenate([t1, t2], axis=2).reshape(MLA_Q_RANK, MLA_HEADS * MLA_ROPE_DIM).astype(BF16)
    wq_s = jnp.concatenate([-t2, t1], axis=2).reshape(MLA_Q_RANK, MLA_HEADS * MLA_ROPE_DIM).astype(BF16)
    inv = 1.0 / (ROPE_THETA ** (jnp.arange(0, MLA_ROPE_DIM, 2, dtype=F32) / MLA_ROPE_DIM))
    inv128 = jnp.tile(inv, 128 // half).reshape(1, 128)
    q, k, v = _mla_prep(c_q, c_kv, k_r2, pos_col, inv128, g_cq.reshape(1, -1), g_ckv.reshape(1, -1),
                        wq_n, wq_r, wq_s, w_ukv.astype(BF16), batch, seq)
    mla_out = _mla_flash(q, k, v).reshape(batch * seq, MLA_V_W)

    x1 = _mix(x2, g_mix.reshape(1, D), swa_out, mla_out, w_ga.astype(BF16), w_gb.astype(BF16),
              w_a_proj.astype(BF16), w_b_proj.astype(BF16), w_o.astype(BF16))

    keys = peer_keys.reshape(PEER_HEADS * 2, PEER_N_KEYS, PEER_HALF).astype(BF16)
    h2, r2, n1, e1, e2 = _peer_route(x1, g_ffn.reshape(1, D), w_peer_q.astype(BF16), keys)
    peer_out = _peer_dense(h2, peer_u.astype(BF16), peer_v.astype(BF16).T, r2, n1, e1, e2)
    return x1, peer_out


def kernel(x, positions, g_mix, w_in, sinks, g_cq, w_uq, g_ckv, w_ukv, w_a_proj, w_b_proj, w_o, g_ffn, w_peer_q,
           peer_keys, peer_u, peer_v, g_final):
    batch, seq, D = x.shape
    depth = g_mix.shape[0]
    assert seq % 1024 == 0 and D % 512 == 0, (seq, D)
    x2 = x.reshape(batch * seq, D)
    pos_f = positions.astype(F32)
    pos_col = pos_f.reshape(batch * seq, 1)
    pos_row = pos_f.reshape(1, batch * seq)
    peer_out = None
    for l in range(depth):
        if peer_out is not None:
            x2 = x2 + peer_out
        x2, peer_out = _layer(x2, pos_col, pos_row, batch, seq, g_mix[l], w_in[l], sinks[l], g_cq[l], w_uq[l],
                              g_ckv[l], w_ukv[l], w_a_proj[l], w_b_proj[l], w_o[l], g_ffn[l], w_peer_q[l],
                              peer_keys[l], peer_u[l], peer_v[l])
    out = _final_norm(x2, peer_out, g_final.reshape(1, D))
    return out.reshape(batch, seq, D)
```

```python
import functools
import math

import numpy as np
import jax
import jax.numpy as jnp
from jax import lax
from jax.experimental import pallas as pl
from jax.experimental.pallas import tpu as pltpu

F32 = jnp.float32
BF16 = jnp.bfloat16

EPS = 1e-6
NEG = -1e30
BLOCK = 128
SWA_HEADS = 16
SWA_KV_HEADS = 4
SWA_GROUP = SWA_HEADS // SWA_KV_HEADS
SWA_HEAD_DIM = 64
SWA_WINDOW = 128
SWA_Q_W = SWA_HEADS * SWA_HEAD_DIM
SWA_KV_W = SWA_KV_HEADS * SWA_HEAD_DIM
MLA_HEADS = 8
MLA_Q_RANK = 512
MLA_KV_RANK = 256
MLA_NOPE_DIM = 128
MLA_ROPE_DIM = 64
MLA_QK_DIM = MLA_NOPE_DIM + MLA_ROPE_DIM
MLA_V_DIM = 128
MLA_V_W = MLA_HEADS * MLA_V_DIM
ROPE_THETA = 10000.0
PEER_HEADS = 8
PEER_N_KEYS = 128
PEER_N_EXPERTS = PEER_N_KEYS * PEER_N_KEYS
PEER_HALF = 128
PEER_TOPK = 16

V7X_VMEM_BYTES = 64 * 1024 * 1024
VMEM_LIMIT = V7X_VMEM_BYTES - 8 * 1024 * 1024


def _params(*semantics):
    return pltpu.CompilerParams(dimension_semantics=semantics, vmem_limit_bytes=VMEM_LIMIT)


def _rms(xf, g):
    y = xf * lax.rsqrt(jnp.mean(xf * xf, axis=-1, keepdims=True) + EPS)
    return y * g


def _dot(a, b):
    return jnp.dot(a, b, preferred_element_type=F32)


def _dot_nt(a, b):
    return lax.dot_general(a, b, (((1,), (1,)), ((), ())), preferred_element_type=F32)


ATTN_SPLITS = (SWA_Q_W, MLA_Q_RANK, SWA_KV_W, SWA_KV_W, MLA_KV_RANK, 2 * MLA_ROPE_DIM)
ATTN_W = sum(ATTN_SPLITS)
LOG2E = math.log2(math.e)


def _attn_inproj_kernel(x_ref, g_ref, w_ref, q_out, cq_out, k_out, v_out, ckv_out, kr_out):
    h = _rms(x_ref[...], g_ref[...]).astype(BF16)
    z = _dot(h, w_ref[...]).astype(BF16)
    d = SWA_HEAD_DIM
    off = 0
    for hd in range(SWA_HEADS):
        q_out[hd] = z[:, off + hd * d:off + (hd + 1) * d]
    off += SWA_Q_W
    cq_out[...] = z[:, off:off + MLA_Q_RANK]
    off += MLA_Q_RANK
    for hd in range(SWA_KV_HEADS):
        k_out[hd] = z[:, off + hd * d:off + (hd + 1) * d]
    off += SWA_KV_W
    for hd in range(SWA_KV_HEADS):
        v_out[hd] = z[:, off + hd * d:off + (hd + 1) * d]
    off += SWA_KV_W
    ckv_out[...] = z[:, off:off + MLA_KV_RANK]
    off += MLA_KV_RANK
    kr_out[...] = z[:, off:off + 2 * MLA_ROPE_DIM]


def _attn_inproj(x2, g, w_attn, tm=512):
    T, D = x2.shape
    d = SWA_HEAD_DIM
    row = lambda i: (i, 0)
    head_major = lambda i: (0, i, 0)
    return pl.pallas_call(
        _attn_inproj_kernel,
        grid=(T // tm,),
        in_specs=[
            pl.BlockSpec((tm, D), row),
            pl.BlockSpec((1, D), lambda i: (0, 0)),
            pl.BlockSpec((D, ATTN_W), lambda i: (0, 0)),
        ],
        out_specs=[
            pl.BlockSpec((SWA_HEADS, tm, d), head_major),
            pl.BlockSpec((tm, MLA_Q_RANK), row),
            pl.BlockSpec((SWA_KV_HEADS, tm, d), head_major),
            pl.BlockSpec((SWA_KV_HEADS, tm, d), head_major),
            pl.BlockSpec((tm, MLA_KV_RANK), row),
            pl.BlockSpec((tm, 2 * MLA_ROPE_DIM), row),
        ],
        out_shape=[
            jax.ShapeDtypeStruct((SWA_HEADS, T, d), BF16),
            jax.ShapeDtypeStruct((T, MLA_Q_RANK), BF16),
            jax.ShapeDtypeStruct((SWA_KV_HEADS, T, d), BF16),
            jax.ShapeDtypeStruct((SWA_KV_HEADS, T, d), BF16),
            jax.ShapeDtypeStruct((T, MLA_KV_RANK), BF16),
            jax.ShapeDtypeStruct((T, 2 * MLA_ROPE_DIM), BF16),
        ],
        compiler_params=_params("parallel"),
        name="attn_inproj",
    )(x2, g, w_attn)


SWA_TILE = 512
SWA_SUB = SWA_TILE // BLOCK


def _swa_kernel(tiles_per_seq, sinks_ref, q_ref, kc_ref, kp_ref, vc_ref, vp_ref, pc_ref, prc_ref, prp_ref,
                o_ref, kbuf, vbuf, pkbuf):
    i = pl.program_id(0)
    kbuf[:, 0:BLOCK, :] = kp_ref[...]
    kbuf[:, BLOCK:, :] = kc_ref[...]
    vbuf[:, 0:BLOCK, :] = vp_ref[...]
    vbuf[:, BLOCK:, :] = vc_ref[...]
    pkbuf[:, 0:BLOCK] = prp_ref[...]
    pkbuf[:, BLOCK:] = prc_ref[...]
    first_tile = (i % tiles_per_seq) == 0

    qi = lax.broadcasted_iota(jnp.int32, (BLOCK, 2 * BLOCK), 0)
    kj = lax.broadcasted_iota(jnp.int32, (BLOCK, 2 * BLOCK), 1)
    rel = BLOCK + qi - kj
    band = (rel >= 0) & (rel < SWA_WINDOW)
    d = SWA_HEAD_DIM
    ones_rhs = jnp.ones((2 * BLOCK, 128), BF16)

    for c in range(SWA_SUB):
        r0 = c * BLOCK
        pq = pc_ref[r0:r0 + BLOCK, :]
        pk = pkbuf[:, r0:r0 + 2 * BLOCK]
        dist = jnp.abs(pq - pk) * LOG2E
        if c == 0:
            mask = band & (jnp.logical_not(first_tile) | (kj >= BLOCK))
        else:
            mask = band
        for g in range(SWA_KV_HEADS):
            qs = jnp.concatenate([q_ref[g * SWA_GROUP + hh, r0:r0 + BLOCK, :] for hh in range(SWA_GROUP)], axis=0)
            logits = _dot_nt(qs, kbuf[g, r0:r0 + 2 * BLOCK, :])
            probs, denoms = [], []
            for hh in range(SWA_GROUP):
                h = g * SWA_GROUP + hh
                slope = 2.0 ** (-8.0 * (h + 1) / SWA_HEADS)
                lh = logits[hh * BLOCK:(hh + 1) * BLOCK] - slope * dist
                lh = jnp.where(mask, lh, NEG)
                sink = sinks_ref[h] * LOG2E
                m = jnp.maximum(jnp.max(lh, axis=-1, keepdims=True), sink)
                p = jnp.exp2(lh - m)
                denoms.append(jnp.exp2(sink - m))
                probs.append(p.astype(BF16))
            pm = jnp.concatenate(probs, axis=0)
            o = _dot(pm, vbuf[g, r0:r0 + 2 * BLOCK, :])
            row_sum = _dot(pm, ones_rhs)
            for hh in range(SWA_GROUP):
                h = g * SWA_GROUP + hh
                denom = row_sum[hh * BLOCK:(hh + 1) * BLOCK, :d] + denoms[hh]
                o_ref[r0:r0 + BLOCK, h * d:(h + 1) * d] = (o[hh * BLOCK:(hh + 1) * BLOCK] / denom).astype(o_ref.dtype)


def _swa(q_h, k_h, v_h, pos_col, pos_row, sinks, seq):
    T = q_h.shape[1]
    d = SWA_HEAD_DIM
    tiles_per_seq = seq // SWA_TILE

    def prev_blk(i):
        return jnp.where(i % tiles_per_seq == 0, i * SWA_SUB, i * SWA_SUB - 1)

    cur = lambda i: (0, i, 0)
    prev = lambda i: (0, prev_blk(i), 0)
    return pl.pallas_call(
        functools.partial(_swa_kernel, tiles_per_seq),
        grid=(T // SWA_TILE,),
        in_specs=[
            pl.BlockSpec(memory_space=pltpu.SMEM),
            pl.BlockSpec((SWA_HEADS, SWA_TILE, d), cur),
            pl.BlockSpec((SWA_KV_HEADS, SWA_TILE, d), cur),
            pl.BlockSpec((SWA_KV_HEADS, BLOCK, d), prev),
            pl.BlockSpec((SWA_KV_HEADS, SWA_TILE, d), cur),
            pl.BlockSpec((SWA_KV_HEADS, BLOCK, d), prev),
            pl.BlockSpec((SWA_TILE, 1), lambda i: (i, 0)),
            pl.BlockSpec((1, SWA_TILE), lambda i: (0, i)),
            pl.BlockSpec((1, BLOCK), lambda i: (0, prev_blk(i))),
        ],
        out_specs=pl.BlockSpec((SWA_TILE, SWA_Q_W), lambda i: (i, 0)),
        out_shape=jax.ShapeDtypeStruct((T, SWA_Q_W), BF16),
        scratch_shapes=[
            pltpu.VMEM((SWA_KV_HEADS, SWA_TILE + BLOCK, d), BF16),
            pltpu.VMEM((SWA_KV_HEADS, SWA_TILE + BLOCK, d), BF16),
            pltpu.VMEM((1, SWA_TILE + BLOCK), F32),
        ],
        compiler_params=_params("parallel"),
        name="swa",
    )(sinks, q_h, k_h, k_h, v_h, v_h, pos_col, pos_row, pos_row)


MLA_VP = 2 * MLA_V_DIM


def _mla_prep_kernel(cq_ref, ckv_ref, kr_ref, pos_ref, inv_ref, gcq_ref, gckv_ref, wqn_ref, wqr_ref, wqs_ref,
                     wkv_ref, q_out, k_out, v_out):
    scale = MLA_QK_DIM ** -0.5 * math.log2(math.e)
    ang = pos_ref[...] * inv_ref[...]
    cos = jnp.cos(ang)
    sin = jnp.sin(ang)
    cos4 = jnp.concatenate([cos] * 4, axis=1)
    sin4 = jnp.concatenate([sin] * 4, axis=1)

    cn = _rms(cq_ref[...].astype(F32), gcq_ref[...]).astype(BF16)
    qn = _dot(cn, wqn_ref[...])
    qr = _dot(cn, wqr_ref[...])
    qs = _dot(cn, wqs_ref[...])
    qrot = qr * cos4 + qs * sin4

    kvn = _rms(ckv_ref[...].astype(F32), gckv_ref[...]).astype(BF16)
    kv = _dot(kvn, wkv_ref[...])
    kr = kr_ref[...].astype(F32)
    krot = (kr[:, :MLA_ROPE_DIM] * cos[:, :MLA_ROPE_DIM]
            + kr[:, MLA_ROPE_DIM:] * sin[:, :MLA_ROPE_DIM]).astype(BF16)

    lane = lax.broadcasted_iota(jnp.int32, (kr.shape[0], MLA_VP - MLA_V_DIM), 1)
    ones_col = jnp.where(lane == 0, 1.0, 0.0).astype(BF16)
    for h in range(MLA_HEADS):
        q_h = jnp.concatenate(
            [qn[:, h * MLA_NOPE_DIM:(h + 1) * MLA_NOPE_DIM], qrot[:, h * MLA_ROPE_DIM:(h + 1) * MLA_ROPE_DIM]],
            axis=1) * scale
        q_out[0, h] = q_h.astype(BF16)
        kn_h = kv[:, h * 256:h * 256 + MLA_NOPE_DIM].astype(BF16)
        k_out[0, h] = jnp.concatenate([kn_h, krot], axis=1)
        v_out[0, h, :, :MLA_V_DIM] = kv[:, h * 256 + MLA_NOPE_DIM:(h + 1) * 256].astype(BF16)
        v_out[0, h, :, MLA_V_DIM:] = ones_col


def _mla_prep(c_q, c_kv, k_r2, pos_col, inv128, g_cq, g_ckv, wq_n, wq_r, wq_s, w_kv, batch, seq, tm=512):
    T = c_q.shape[0]
    tps = seq // tm
    const = lambda i: (0, 0)
    out_idx = lambda i: (i // tps, 0, i % tps, 0)
    return pl.pallas_call(
        _mla_prep_kernel,
        grid=(T // tm,),
        in_specs=[
            pl.BlockSpec((tm, MLA_Q_RANK), lambda i: (i, 0)),
            pl.BlockSpec((tm, MLA_KV_RANK), lambda i: (i, 0)),
            pl.BlockSpec((tm, 2 * MLA_ROPE_DIM), lambda i: (i, 0)),
            pl.BlockSpec((tm, 1), lambda i: (i, 0)),
            pl.BlockSpec((1, 128), const),
            pl.BlockSpec((1, MLA_Q_RANK), const),
            pl.BlockSpec((1, MLA_KV_RANK), const),
            pl.BlockSpec(wq_n.shape, const),
            pl.BlockSpec(wq_r.shape, const),
            pl.BlockSpec(wq_s.shape, const),
            pl.BlockSpec(w_kv.shape, const),
        ],
        out_specs=[
            pl.BlockSpec((1, MLA_HEADS, tm, MLA_QK_DIM), out_idx),
            pl.BlockSpec((1, MLA_HEADS, tm, MLA_QK_DIM), out_idx),
            pl.BlockSpec((1, MLA_HEADS, tm, MLA_VP), out_idx),
        ],
        out_shape=[
            jax.ShapeDtypeStruct((batch, MLA_HEADS, seq, MLA_QK_DIM), BF16),
            jax.ShapeDtypeStruct((batch, MLA_HEADS, seq, MLA_QK_DIM), BF16),
            jax.ShapeDtypeStruct((batch, MLA_HEADS, seq, MLA_VP), BF16),
        ],
        compiler_params=_params("parallel"),
        name="mla_prep",
    )(c_q, c_kv, k_r2, pos_col, inv128, g_cq, g_ckv, wq_n, wq_r, wq_s, w_kv)


MLA_TILE = 512
MLA_HP = 2


def _mla_flash_kernel(q_ref, k_ref, v_ref, o_ref, s_sc, p_sc, alpha_sc, m_sc, acc_sc):
    qi = pl.program_id(2)
    t = q_ref.shape[2]
    heads = range(MLA_HP)

    def scores(hh, j):
        r0 = pl.multiple_of(j * t, t)
        return _dot_nt(q_ref[0, hh], k_ref[0, hh, pl.ds(r0, t), :])

    def softmax(hh, s, slot):
        m_prev = m_sc[hh]
        m_new = jnp.maximum(m_prev, jnp.max(s, axis=1, keepdims=True))
        alpha_sc[slot, hh] = jnp.exp2(m_prev - m_new)
        p_sc[slot, hh] = jnp.exp2(s - jnp.concatenate([m_new] * (t // 128), axis=1)).astype(BF16)
        m_sc[hh] = m_new

    def accumulate(hh, j, slot):
        r0 = pl.multiple_of(j * t, t)
        pv = _dot(p_sc[slot, hh], v_ref[0, hh, pl.ds(r0, t), :])
        acc_sc[hh] = jnp.concatenate([alpha_sc[slot, hh]] * (MLA_VP // 128), axis=1) * acc_sc[hh] + pv

    m_sc[...] = jnp.full(m_sc.shape, NEG, F32)
    acc_sc[...] = jnp.zeros(acc_sc.shape, F32)
    p_sc[1] = jnp.zeros(p_sc.shape[1:], BF16)
    alpha_sc[1] = jnp.ones(alpha_sc.shape[1:], F32)
    for hh in heads:
        s_sc[0, hh] = scores(hh, 0)

    def stage(j, slot):
        for hh in heads:
            s_sc[1 - slot, hh] = scores(hh, j + 1)
            softmax(hh, s_sc[slot, hh], slot)
            accumulate(hh, jnp.maximum(j - 1, 0), 1 - slot)

    def body(i, carry):
        stage(2 * i, 0)
        stage(2 * i + 1, 1)
        return carry

    lax.fori_loop(0, qi // 2, body, 0)

    def finish(slot):
        row = lax.broadcasted_iota(jnp.int32, (t, t), 0)
        col = lax.broadcasted_iota(jnp.int32, (t, t), 1)
        for hh in heads:
            softmax(hh, jnp.where(col <= row, s_sc[slot, hh], NEG), slot)
            accumulate(hh, jnp.maximum(qi - 1, 0), 1 - slot)
        for hh in heads:
            accumulate(hh, qi, slot)
            acc = acc_sc[hh]
            o_ref[0, :, hh * MLA_V_DIM:(hh + 1) * MLA_V_DIM] = (
                acc[:, :MLA_V_DIM] / acc[:, MLA_V_DIM:MLA_V_DIM + 1]).astype(o_ref.dtype)

    @pl.when(qi % 2 == 0)
    def _():
        finish(0)

    @pl.when(qi % 2 == 1)
    def _():
        stage(qi - 1, 0)
        finish(1)


def _mla_flash(q, k, v):
    B, H, S, _ = q.shape
    t = MLA_TILE
    hp = MLA_HP
    return pl.pallas_call(
        _mla_flash_kernel,
        grid=(B, H // hp, S // t),
        in_specs=[
            pl.BlockSpec((1, hp, t, MLA_QK_DIM), lambda b, h, i: (b, h, i, 0)),
            pl.BlockSpec((1, hp, S, MLA_QK_DIM), lambda b, h, i: (b, h, 0, 0)),
            pl.BlockSpec((1, hp, S, MLA_VP), lambda b, h, i: (b, h, 0, 0)),
        ],
        out_specs=pl.BlockSpec((1, t, hp * MLA_V_DIM), lambda b, h, i: (b, i, h)),
        out_shape=jax.ShapeDtypeStruct((B, S, MLA_V_W), BF16),
        scratch_shapes=[
            pltpu.VMEM((2, hp, t, t), F32),
            pltpu.VMEM((2, hp, t, t), BF16),
            pltpu.VMEM((2, hp, t, 128), F32),
            pltpu.VMEM((hp, t, 128), F32),
            pltpu.VMEM((hp, t, MLA_VP), F32),
        ],
        compiler_params=_params("parallel", "parallel", "arbitrary"),
        name="mla_flash",
    )(q, k, v)


def _mix_kernel(x_ref, g_ref, ya_in_ref, yb_in_ref, wga_ref, wgb_ref, wa_ref, wb_ref, wo_ref, o_ref, h_sc, acc_sc):
    n = pl.program_id(1)

    @pl.when(n == 0)
    def _():
        h_sc[...] = _rms(x_ref[...], g_ref[...]).astype(BF16)
        acc_sc[...] = jnp.zeros(acc_sc.shape, F32)

    h = h_sc[...]
    ga = _dot(h, wga_ref[...])
    gb = _dot(h, wgb_ref[...])
    ya = _dot(ya_in_ref[...], wa_ref[...])
    yb = _dot(yb_in_ref[...], wb_ref[...])
    mixed = jax.nn.sigmoid(ga) * ya + jax.nn.sigmoid(gb) * yb
    acc_sc[...] += _dot(mixed.astype(BF16), wo_ref[...])

    @pl.when(n == pl.num_programs(1) - 1)
    def _():
        o_ref[...] = x_ref[...] + acc_sc[...]


def _mix(x2, g, swa_out, mla_out, w_ga, w_gb, w_a, w_b, w_o, tm=512, tn=512):
    T, D = x2.shape
    return pl.pallas_call(
        _mix_kernel,
        grid=(T // tm, D // tn),
        in_specs=[
            pl.BlockSpec((tm, D), lambda i, n: (i, 0)),
            pl.BlockSpec((1, D), lambda i, n: (0, 0)),
            pl.BlockSpec((tm, SWA_Q_W), lambda i, n: (i, 0)),
            pl.BlockSpec((tm, MLA_V_W), lambda i, n: (i, 0)),
            pl.BlockSpec((D, tn), lambda i, n: (0, n)),
            pl.BlockSpec((D, tn), lambda i, n: (0, n)),
            pl.BlockSpec((SWA_Q_W, tn), lambda i, n: (0, n)),
            pl.BlockSpec((MLA_V_W, tn), lambda i, n: (0, n)),
            pl.BlockSpec((tn, D), lambda i, n: (n, 0)),
        ],
        out_specs=pl.BlockSpec((tm, D), lambda i, n: (i, 0)),
        out_shape=jax.ShapeDtypeStruct((T, D), F32),
        scratch_shapes=[pltpu.VMEM((tm, D), BF16), pltpu.VMEM((tm, D), F32)],
        compiler_params=_params("parallel", "arbitrary"),
        name="mix",
    )(x2, g, swa_out, mla_out, w_ga, w_gb, w_a, w_b, w_o)


PEER_LANES = 128
CAND_ROWS = 16 + 7 * 8 + 8


def _extract_top(s, iters, exact):
    n_rows = s.shape[0]
    rows = lax.broadcasted_iota(jnp.int32, s.shape, 0)
    rank = jnp.full(s.shape, float(iters), F32)
    vals = []
    for r in range(iters):
        m = jnp.max(s, axis=0, keepdims=True)
        if exact:
            idx = jnp.min(jnp.where(s == m, rows, n_rows), axis=0, keepdims=True)
            sel = rows == idx
        else:
            sel = s == m
        rank = jnp.where(sel, float(r), rank)
        s = jnp.where(sel, -jnp.inf, s)
        vals.append(m)
    return rank, vals


def _ranked_excess(rank, iters):
    return jnp.sum(jnp.where(rank < float(iters), 1.0, 0.0), axis=0, keepdims=True) - float(iters)


def _peer_route_kernel(x_ref, g_ref, wq_ref, keys_ref, h_out, r2_out, n1_out, e1_out, e2_out, q_sc):
    tm = x_ref.shape[0]
    h = _rms(x_ref[...], g_ref[...]).astype(BF16)
    h_out[...] = h
    q_sc[...] = _dot(h, wq_ref[...]).astype(BF16)

    def route(hd, t0, exact):
        c0 = pl.multiple_of(hd * 2 * PEER_HALF, 2 * PEER_HALF)
        q1 = q_sc[t0:t0 + PEER_LANES, pl.ds(c0, PEER_HALF)]
        q2 = q_sc[t0:t0 + PEER_LANES, pl.ds(c0 + PEER_HALF, PEER_HALF)]
        s1 = _dot_nt(keys_ref[2 * hd], q1)
        s2 = _dot_nt(keys_ref[2 * hd + 1], q2)
        r1, v1 = _extract_top(s1, PEER_TOPK, exact)
        r2, v2 = _extract_top(s2, PEER_TOPK, exact)
        v2_lo = jnp.concatenate(v2[:8], axis=0)
        v2_all = jnp.concatenate(v2, axis=0)
        slabs = [v1[0] + v2_all]
        for a in range(1, 8):
            slabs.append(v1[a] + v2_lo)
        slabs.append(jnp.concatenate(v1[8:], axis=0) + v2[0])
        cand = jnp.concatenate(slabs, axis=0)
        rc, vc = _extract_top(cand, PEER_TOPK, exact)
        picked = jnp.where(rc < float(PEER_TOPK), 1.0, 0.0)
        z = jnp.ones_like(vc[0])
        for r in range(1, PEER_TOPK):
            z = z + jnp.exp(vc[r] - vc[0])
        cnt_lo = picked[0:8]
        for a in range(1, 8):
            cnt_lo = cnt_lo + picked[8 + 8 * a:16 + 8 * a]
        first_col = jnp.sum(picked[72:80], axis=0, keepdims=True)
        cnt_hi = jnp.sum(picked[8:16], axis=0, keepdims=True)
        n1 = jnp.where(r1 == 0.0, cnt_hi, 0.0)
        for b in range(8):
            cnt_b = cnt_lo[b:b + 1] + first_col if b == 0 else cnt_lo[b:b + 1]
            n1 = n1 + jnp.where(r1 < cnt_b, 1.0, 0.0)
        r2_out[hd, :, t0:t0 + PEER_LANES] = r2.astype(r2_out.dtype)
        n1_out[hd, :, t0:t0 + PEER_LANES] = n1
        e1_out[hd, :, t0:t0 + PEER_LANES] = jnp.exp(s1 - v1[0])
        e2_out[hd, :, t0:t0 + PEER_LANES] = (jnp.exp(s2 - v2[0]) / z).astype(e2_out.dtype)
        if exact:
            return None
        excess = (_ranked_excess(r1, PEER_TOPK) + _ranked_excess(r2, PEER_TOPK)
                  + _ranked_excess(rc, PEER_TOPK))
        return jnp.max(excess)

    def head_body(hd, carry):
        starts = [ch * PEER_LANES for ch in range(tm // PEER_LANES)]
        ties = [route(hd, t0, exact=False) for t0 in starts]

        @pl.when(functools.reduce(jnp.maximum, ties) > 0.0)
        def _():
            for t0 in starts:
                route(hd, t0, exact=True)
        return carry

    lax.fori_loop(0, PEER_HEADS, head_body, 0)


def _peer_route(x1, g, w_q, keys, tm=512):
    T, D = x1.shape
    side = jax.ShapeDtypeStruct((PEER_HEADS, PEER_N_KEYS, T), F32)
    side_bf = jax.ShapeDtypeStruct((PEER_HEADS, PEER_N_KEYS, T), BF16)
    side_spec = pl.BlockSpec((PEER_HEADS, PEER_N_KEYS, tm), lambda i: (0, 0, i))
    return pl.pallas_call(
        _peer_route_kernel,
        grid=(T // tm,),
        in_specs=[
            pl.BlockSpec((tm, D), lambda i: (i, 0)),
            pl.BlockSpec((1, D), lambda i: (0, 0)),
            pl.BlockSpec(w_q.shape, lambda i: (0, 0)),
            pl.BlockSpec(keys.shape, lambda i: (0, 0, 0)),
        ],
        out_specs=[pl.BlockSpec((tm, D), lambda i: (i, 0)), side_spec, side_spec, side_spec, side_spec],
        out_shape=[jax.ShapeDtypeStruct((T, D), BF16), side_bf, side, side, side_bf],
        scratch_shapes=[pltpu.VMEM((tm, w_q.shape[1]), BF16)],
        compiler_params=_params("parallel"),
        name="peer_route",
    )(x1, g, w_q, keys)


def _peer_dense_kernel(h_ref, u_ref, vt_ref, r2_ref, n1_ref, e1_ref, e2_ref, o_ref, acc_sc):
    e = pl.program_id(1)
    te = u_ref.shape[0]

    @pl.when(e == 0)
    def _():
        acc_sc[...] = jnp.zeros(acc_sc.shape, F32)

    a_t = _dot_nt(u_ref[...], h_ref[...])
    act = (0.5 * a_t * (1.0 + lax.erf(a_t * math.sqrt(0.5)))).astype(BF16)
    zero = jnp.zeros((), BF16)
    w_rows = []
    for ii in range(te // PEER_N_KEYS):
        i = e * (te // PEER_N_KEYS) + ii
        gate = None
        for hd in range(PEER_HEADS):
            n1 = n1_ref[hd, pl.ds(i, 1), :].astype(BF16)
            e1 = e1_ref[hd, pl.ds(i, 1), :].astype(BF16)
            term = jnp.where(r2_ref[hd] < n1, e2_ref[hd] * e1, zero)
            gate = term if gate is None else gate + term
        w_rows.append(act[ii * PEER_N_KEYS:(ii + 1) * PEER_N_KEYS] * gate)
    w = jnp.concatenate(w_rows, axis=0) if len(w_rows) > 1 else w_rows[0]
    acc_sc[...] += _dot(vt_ref[...], w)

    @pl.when(e == pl.num_programs(1) - 1)
    def _():
        o_ref[...] = acc_sc[...].T.astype(o_ref.dtype)


def _peer_dense(h2, u_bf, vt_bf, r2, n1, e1, e2, tm=1024, te=512):
    T, D = h2.shape
    once = pl.Buffered(1)
    side = (PEER_HEADS, PEER_N_KEYS, tm)
    side_idx = lambda i, e: (0, 0, i)
    return pl.pallas_call(
        _peer_dense_kernel,
        grid=(T // tm, PEER_N_EXPERTS // te),
        in_specs=[
            pl.BlockSpec((tm, D), lambda i, e: (i, 0)),
            pl.BlockSpec((te, D), lambda i, e: (e, 0)),
            pl.BlockSpec((D, te), lambda i, e: (0, e)),
            pl.BlockSpec(side, side_idx),
            pl.BlockSpec(side, side_idx, pipeline_mode=once),
            pl.BlockSpec(side, side_idx, pipeline_mode=once),
            pl.BlockSpec(side, side_idx),
        ],
        out_specs=pl.BlockSpec((tm, D), lambda i, e: (i, 0), pipeline_mode=once),
        out_shape=jax.ShapeDtypeStruct((T, D), BF16),
        scratch_shapes=[pltpu.VMEM((D, tm), F32)],
        compiler_params=_params("parallel", "arbitrary"),
        name="peer_dense",
    )(h2, u_bf, vt_bf, r2, n1, e1, e2)


def _final_kernel(x_ref, p_ref, g_ref, o_ref):
    o_ref[...] = _rms(x_ref[...] + p_ref[...].astype(F32), g_ref[...])


def _final_norm(x1, peer_out, g, tm=512):
    T, D = x1.shape
    return pl.pallas_call(
        _final_kernel,
        grid=(T // tm,),
        in_specs=[
            pl.BlockSpec((tm, D), lambda i: (i, 0)),
            pl.BlockSpec((tm, D), lambda i: (i, 0)),
            pl.BlockSpec((1, D), lambda i: (0, 0)),
        ],
        out_specs=pl.BlockSpec((tm, D), lambda i: (i, 0)),
        out_shape=jax.ShapeDtypeStruct((T, D), F32),
        compiler_params=_params("parallel"),
        name="final_norm",
    )(x1, peer_out, g)


def _layer(x2, pos_col, pos_row, batch, seq, g_mix, w_in, sinks, g_cq, w_uq, g_ckv, w_ukv, w_a_proj, w_b_proj,
           w_o, g_ffn, w_peer_q, peer_keys, peer_u, peer_v):
    D = x2.shape[1]
    o = np.cumsum([0, SWA_Q_W, SWA_KV_W, SWA_KV_W, MLA_Q_RANK, MLA_KV_RANK, MLA_ROPE_DIM, D, D])
    w_qa, w_ka, w_va, w_cq, w_ckv, w_kr, w_ga, w_gb = [w_in[:, o[j]:o[j + 1]] for j in range(8)]
    half = MLA_ROPE_DIM // 2
    w_kr_swapped = jnp.concatenate([-w_kr[:, half:], w_kr[:, :half]], axis=1)
    w_qa = w_qa * (SWA_HEAD_DIM ** -0.5 * LOG2E)
    w_attn = jnp.concatenate([w_qa, w_cq, w_ka, w_va, w_ckv, w_kr, w_kr_swapped], axis=1).astype(BF16)

    q_h, c_q, k_h, v_h, c_kv, k_r2 = _attn_inproj(x2, g_mix.reshape(1, D), w_attn)
    swa_out = _swa(q_h, k_h, v_h, pos_col, pos_row, sinks.astype(F32), seq)

    w_uq3 = w_uq.reshape(MLA_Q_RANK, MLA_HEADS, MLA_QK_DIM)
    wq_n = w_uq3[:, :, :MLA_NOPE_DIM].reshape(MLA_Q_RANK, MLA_HEADS * MLA_NOPE_DIM).astype(BF16)
    t1 = w_uq3[:, :, MLA_NOPE_DIM:MLA_NOPE_DIM + half]
    t2 = w_uq3[:, :, MLA_NOPE_DIM + half:]
    wq_r = jnp.concatenate([t1, t2], axis=2).reshape(MLA_Q_RANK, MLA_HEADS * MLA_ROPE_DIM).astype(BF16)
    wq_s = jnp.concatenate([-t2, t1], axis=2).reshape(MLA_Q_RANK, MLA_HEADS * MLA_ROPE_DIM).astype(BF16)
    inv = 1.0 / (ROPE_THETA ** (jnp.arange(0, MLA_ROPE_DIM, 2, dtype=F32) / MLA_ROPE_DIM))
    inv128 = jnp.tile(inv, 128 // half).reshape(1, 128)
    q, k, v = _mla_prep(c_q, c_kv, k_r2, pos_col, inv128, g_cq.reshape(1, -1), g_ckv.reshape(1, -1),
                        wq_n, wq_r, wq_s, w_ukv.astype(BF16), batch, seq)
    mla_out = _mla_flash(q, k, v).reshape(batch * seq, MLA_V_W)

    x1 = _mix(x2, g_mix.reshape(1, D), swa_out, mla_out, w_ga.astype(BF16), w_gb.astype(BF16),
              w_a_proj.astype(BF16), w_b_proj.astype(BF16), w_o.astype(BF16))

    keys = peer_keys.reshape(PEER_HEADS * 2, PEER_N_KEYS, PEER_HALF).astype(BF16)
    h2, r2, n1, e1, e2 = _peer_route(x1, g_ffn.reshape(1, D), w_peer_q.astype(BF16), keys)
    peer_out = _peer_dense(h2, peer_u.astype(BF16), peer_v.astype(BF16).T, r2, n1, e1, e2)
    return x1, peer_out


def kernel(x, positions, g_mix, w_in, sinks, g_cq, w_uq, g_ckv, w_ukv, w_a_proj, w_b_proj, w_o, g_ffn, w_peer_q,
           peer_keys, peer_u, peer_v, g_final):
    batch, seq, D = x.shape
    depth = g_mix.shape[0]
    assert seq % 1024 == 0 and D % 512 == 0, (seq, D)
    x2 = x.reshape(batch * seq, D)
    pos_f = positions.astype(F32)
    pos_col = pos_f.reshape(batch * seq, 1)
    pos_row = pos_f.reshape(1, batch * seq)
    peer_out = None
    for l in range(depth):
        if peer_out is not None:
            x2 = x2 + peer_out
        x2, peer_out = _layer(x2, pos_col, pos_row, batch, seq, g_mix[l], w_in[l], sinks[l], g_cq[l], w_uq[l],
                              g_ckv[l], w_ukv[l], w_a_proj[l], w_b_proj[l], w_o[l], g_ffn[l], w_peer_q[l],
                              peer_keys[l], peer_u[l], peer_v[l])
    out = _final_norm(x2, peer_out, g_final.reshape(1, D))
    return out.reshape(batch, seq, D)
```

```python
import functools
import math

import numpy as np
import jax
import jax.numpy as jnp
from jax import lax
from jax.experimental import pallas as pl
from jax.experimental.pallas import tpu as pltpu

F32 = jnp.float32
BF16 = jnp.bfloat16

EPS = 1e-6
NEG = -1e30
BLOCK = 128
SWA_HEADS = 16
SWA_KV_HEADS = 4
SWA_GROUP = SWA_HEADS // SWA_KV_HEADS
SWA_HEAD_DIM = 64
SWA_WINDOW = 128
SWA_Q_W = SWA_HEADS * SWA_HEAD_DIM
SWA_KV_W = SWA_KV_HEADS * SWA_HEAD_DIM
MLA_HEADS = 8
MLA_Q_RANK = 512
MLA_KV_RANK = 256
MLA_NOPE_DIM = 128
MLA_ROPE_DIM = 64
MLA_QK_DIM = MLA_NOPE_DIM + MLA_ROPE_DIM
MLA_V_DIM = 128
MLA_V_W = MLA_HEADS * MLA_V_DIM
ROPE_THETA = 10000.0
PEER_HEADS = 8
PEER_N_KEYS = 128
PEER_N_EXPERTS = PEER_N_KEYS * PEER_N_KEYS
PEER_HALF = 128
PEER_TOPK = 16

V7X_VMEM_BYTES = 64 * 1024 * 1024
VMEM_LIMIT = V7X_VMEM_BYTES - 8 * 1024 * 1024


def _params(*semantics):
    return pltpu.CompilerParams(dimension_semantics=semantics, vmem_limit_bytes=VMEM_LIMIT)


def _rms(xf, g):
    y = xf * lax.rsqrt(jnp.mean(xf * xf, axis=-1, keepdims=True) + EPS)
    return y * g


def _dot(a, b):
    return jnp.dot(a, b, preferred_element_type=F32)


def _dot_nt(a, b):
    return lax.dot_general(a, b, (((1,), (1,)), ((), ())), preferred_element_type=F32)


ATTN_COLS = SWA_Q_W + 2 * SWA_KV_W + MLA_Q_RANK + MLA_KV_RANK + MLA_ROPE_DIM
ATTN_WIN = -(-ATTN_COLS // 128) * 128
LOG2E = math.log2(math.e)


def _attn_inproj_kernel(x_ref, g_ref, w_ref, q_out, cq_out, k_out, v_out, ckv_out, kr_out, w_sc):
    @pl.when(pl.program_id(0) == 0)
    def _():
        w_sc[...] = w_ref[...].astype(BF16)

    h = _rms(x_ref[...], g_ref[...]).astype(BF16)
    z = _dot(h, w_sc[...])
    d = SWA_HEAD_DIM
    off = 0
    q = (z[:, :SWA_Q_W] * (d ** -0.5 * LOG2E)).astype(BF16)
    for hd in range(SWA_HEADS):
        q_out[hd] = q[:, hd * d:(hd + 1) * d]
    off += SWA_Q_W
    for hd in range(SWA_KV_HEADS):
        k_out[hd] = z[:, off + hd * d:off + (hd + 1) * d].astype(BF16)
    off += SWA_KV_W
    for hd in range(SWA_KV_HEADS):
        v_out[hd] = z[:, off + hd * d:off + (hd + 1) * d].astype(BF16)
    off += SWA_KV_W
    cq_out[...] = z[:, off:off + MLA_Q_RANK].astype(BF16)
    off += MLA_Q_RANK
    ckv_out[...] = z[:, off:off + MLA_KV_RANK].astype(BF16)
    off += MLA_KV_RANK
    half = MLA_ROPE_DIM // 2
    t1 = z[:, off:off + half]
    t2 = z[:, off + half:off + MLA_ROPE_DIM]
    kr_out[...] = jnp.concatenate([t1, t2, -t2, t1], axis=1).astype(BF16)


def _attn_inproj(x2, g, w_in, tm=512):
    T, D = x2.shape
    d = SWA_HEAD_DIM
    row = lambda i: (i, 0)
    head_major = lambda i: (0, i, 0)
    return pl.pallas_call(
        _attn_inproj_kernel,
        grid=(T // tm,),
        in_specs=[
            pl.BlockSpec((tm, D), row),
            pl.BlockSpec((1, D), lambda i: (0, 0)),
            pl.BlockSpec((D, ATTN_WIN), lambda i: (0, 0), pipeline_mode=pl.Buffered(1)),
        ],
        out_specs=[
            pl.BlockSpec((SWA_HEADS, tm, d), head_major),
            pl.BlockSpec((tm, MLA_Q_RANK), row),
            pl.BlockSpec((SWA_KV_HEADS, tm, d), head_major),
            pl.BlockSpec((SWA_KV_HEADS, tm, d), head_major),
            pl.BlockSpec((tm, MLA_KV_RANK), row),
            pl.BlockSpec((tm, 2 * MLA_ROPE_DIM), row),
        ],
        out_shape=[
            jax.ShapeDtypeStruct((SWA_HEADS, T, d), BF16),
            jax.ShapeDtypeStruct((T, MLA_Q_RANK), BF16),
            jax.ShapeDtypeStruct((SWA_KV_HEADS, T, d), BF16),
            jax.ShapeDtypeStruct((SWA_KV_HEADS, T, d), BF16),
            jax.ShapeDtypeStruct((T, MLA_KV_RANK), BF16),
            jax.ShapeDtypeStruct((T, 2 * MLA_ROPE_DIM), BF16),
        ],
        scratch_shapes=[pltpu.VMEM((D, ATTN_WIN), BF16)],
        compiler_params=_params("arbitrary"),
        name="attn_inproj",
    )(x2, g, w_in)


SWA_TILE = 512
SWA_SUB = SWA_TILE // BLOCK


def _swa_kernel(tiles_per_seq, sinks_ref, q_ref, kc_ref, kp_ref, vc_ref, vp_ref, pc_ref, prc_ref, prp_ref,
                o_ref, kbuf, vbuf, pkbuf):
    i = pl.program_id(0)
    kbuf[:, 0:BLOCK, :] = kp_ref[...]
    kbuf[:, BLOCK:, :] = kc_ref[...]
    vbuf[:, 0:BLOCK, :] = vp_ref[...]
    vbuf[:, BLOCK:, :] = vc_ref[...]
    pkbuf[:, 0:BLOCK] = prp_ref[...]
    pkbuf[:, BLOCK:] = prc_ref[...]
    first_tile = (i % tiles_per_seq) == 0

    qi = lax.broadcasted_iota(jnp.int32, (BLOCK, 2 * BLOCK), 0)
    kj = lax.broadcasted_iota(jnp.int32, (BLOCK, 2 * BLOCK), 1)
    rel = BLOCK + qi - kj
    band = (rel >= 0) & (rel < SWA_WINDOW)
    d = SWA_HEAD_DIM
    ones_rhs = jnp.ones((2 * BLOCK, 128), BF16)

    for c in range(SWA_SUB):
        r0 = c * BLOCK
        pq = pc_ref[r0:r0 + BLOCK, :]
        pk = pkbuf[:, r0:r0 + 2 * BLOCK]
        dist = jnp.abs(pq - pk) * LOG2E
        if c == 0:
            mask = band & (jnp.logical_not(first_tile) | (kj >= BLOCK))
        else:
            mask = band
        for g in range(SWA_KV_HEADS):
            qs = jnp.concatenate([q_ref[g * SWA_GROUP + hh, r0:r0 + BLOCK, :] for hh in range(SWA_GROUP)], axis=0)
            logits = _dot_nt(qs, kbuf[g, r0:r0 + 2 * BLOCK, :])
            probs, denoms = [], []
            for hh in range(SWA_GROUP):
                h = g * SWA_GROUP + hh
                slope = 2.0 ** (-8.0 * (h + 1) / SWA_HEADS)
                lh = logits[hh * BLOCK:(hh + 1) * BLOCK] - slope * dist
                lh = jnp.where(mask, lh, NEG)
                sink = sinks_ref[h] * LOG2E
                m = jnp.maximum(jnp.max(lh, axis=-1, keepdims=True), sink)
                p = jnp.exp2(lh - m)
                denoms.append(jnp.exp2(sink - m))
                probs.append(p.astype(BF16))
            pm = jnp.concatenate(probs, axis=0)
            o = _dot(pm, vbuf[g, r0:r0 + 2 * BLOCK, :])
            row_sum = _dot(pm, ones_rhs)
            for hh in range(SWA_GROUP):
                h = g * SWA_GROUP + hh
                denom = row_sum[hh * BLOCK:(hh + 1) * BLOCK, :d] + denoms[hh]
                o_ref[r0:r0 + BLOCK, h * d:(h + 1) * d] = (o[hh * BLOCK:(hh + 1) * BLOCK] / denom).astype(o_ref.dtype)


def _swa(q_h, k_h, v_h, pos_col, pos_row, sinks, seq):
    T = q_h.shape[1]
    d = SWA_HEAD_DIM
    tiles_per_seq = seq // SWA_TILE

    def prev_blk(i):
        return jnp.where(i % tiles_per_seq == 0, i * SWA_SUB, i * SWA_SUB - 1)

    cur = lambda i: (0, i, 0)
    prev = lambda i: (0, prev_blk(i), 0)
    return pl.pallas_call(
        functools.partial(_swa_kernel, tiles_per_seq),
        grid=(T // SWA_TILE,),
        in_specs=[
            pl.BlockSpec(memory_space=pltpu.SMEM),
            pl.BlockSpec((SWA_HEADS, SWA_TILE, d), cur),
            pl.BlockSpec((SWA_KV_HEADS, SWA_TILE, d), cur),
            pl.BlockSpec((SWA_KV_HEADS, BLOCK, d), prev),
            pl.BlockSpec((SWA_KV_HEADS, SWA_TILE, d), cur),
            pl.BlockSpec((SWA_KV_HEADS, BLOCK, d), prev),
            pl.BlockSpec((SWA_TILE, 1), lambda i: (i, 0)),
            pl.BlockSpec((1, SWA_TILE), lambda i: (0, i)),
            pl.BlockSpec((1, BLOCK), lambda i: (0, prev_blk(i))),
        ],
        out_specs=pl.BlockSpec((SWA_TILE, SWA_Q_W), lambda i: (i, 0)),
        out_shape=jax.ShapeDtypeStruct((T, SWA_Q_W), BF16),
        scratch_shapes=[
            pltpu.VMEM((SWA_KV_HEADS, SWA_TILE + BLOCK, d), BF16),
            pltpu.VMEM((SWA_KV_HEADS, SWA_TILE + BLOCK, d), BF16),
            pltpu.VMEM((1, SWA_TILE + BLOCK), F32),
        ],
        compiler_params=_params("parallel"),
        name="swa",
    )(sinks, q_h, k_h, k_h, v_h, v_h, pos_col, pos_row, pos_row)


MLA_VP = 2 * MLA_V_DIM


def _mla_prep_kernel(cq_ref, ckv_ref, kr_ref, pos_ref, inv_ref, gcq_ref, gckv_ref, wqn_ref, wqr_ref, wqs_ref,
                     wkv_ref, q_out, k_out, v_out):
    scale = MLA_QK_DIM ** -0.5 * math.log2(math.e)
    ang = pos_ref[...] * inv_ref[...]
    cos = jnp.cos(ang)
    sin = jnp.sin(ang)
    cos4 = jnp.concatenate([cos] * 4, axis=1)
    sin4 = jnp.concatenate([sin] * 4, axis=1)

    cn = _rms(cq_ref[...].astype(F32), gcq_ref[...]).astype(BF16)
    qn = _dot(cn, wqn_ref[...])
    qr = _dot(cn, wqr_ref[...])
    qs = _dot(cn, wqs_ref[...])
    qrot = qr * cos4 + qs * sin4

    kvn = _rms(ckv_ref[...].astype(F32), gckv_ref[...]).astype(BF16)
    kv = _dot(kvn, wkv_ref[...])
    kr = kr_ref[...].astype(F32)
    krot = (kr[:, :MLA_ROPE_DIM] * cos[:, :MLA_ROPE_DIM]
            + kr[:, MLA_ROPE_DIM:] * sin[:, :MLA_ROPE_DIM]).astype(BF16)

    lane = lax.broadcasted_iota(jnp.int32, (kr.shape[0], MLA_VP - MLA_V_DIM), 1)
    ones_col = jnp.where(lane == 0, 1.0, 0.0).astype(BF16)
    for h in range(MLA_HEADS):
        q_h = jnp.concatenate(
            [qn[:, h * MLA_NOPE_DIM:(h + 1) * MLA_NOPE_DIM], qrot[:, h * MLA_ROPE_DIM:(h + 1) * MLA_ROPE_DIM]],
            axis=1) * scale
        q_out[0, h] = q_h.astype(BF16)
        kn_h = kv[:, h * 256:h * 256 + MLA_NOPE_DIM].astype(BF16)
        k_out[0, h] = jnp.concatenate([kn_h, krot], axis=1)
        v_out[0, h, :, :MLA_V_DIM] = kv[:, h * 256 + MLA_NOPE_DIM:(h + 1) * 256].astype(BF16)
        v_out[0, h, :, MLA_V_DIM:] = ones_col


def _mla_prep(c_q, c_kv, k_r2, pos_col, inv128, g_cq, g_ckv, wq_n, wq_r, wq_s, w_kv, batch, seq, tm=512):
    T = c_q.shape[0]
    tps = seq // tm
    const = lambda i: (0, 0)
    out_idx = lambda i: (i // tps, 0, i % tps, 0)
    return pl.pallas_call(
        _mla_prep_kernel,
        grid=(T // tm,),
        in_specs=[
            pl.BlockSpec((tm, MLA_Q_RANK), lambda i: (i, 0)),
            pl.BlockSpec((tm, MLA_KV_RANK), lambda i: (i, 0)),
            pl.BlockSpec((tm, 2 * MLA_ROPE_DIM), lambda i: (i, 0)),
            pl.BlockSpec((tm, 1), lambda i: (i, 0)),
            pl.BlockSpec((1, 128), const),
            pl.BlockSpec((1, MLA_Q_RANK), const),
            pl.BlockSpec((1, MLA_KV_RANK), const),
            pl.BlockSpec(wq_n.shape, const),
            pl.BlockSpec(wq_r.shape, const),
            pl.BlockSpec(wq_s.shape, const),
            pl.BlockSpec(w_kv.shape, const),
        ],
        out_specs=[
            pl.BlockSpec((1, MLA_HEADS, tm, MLA_QK_DIM), out_idx),
            pl.BlockSpec((1, MLA_HEADS, tm, MLA_QK_DIM), out_idx),
            pl.BlockSpec((1, MLA_HEADS, tm, MLA_VP), out_idx),
        ],
        out_shape=[
            jax.ShapeDtypeStruct((batch, MLA_HEADS, seq, MLA_QK_DIM), BF16),
            jax.ShapeDtypeStruct((batch, MLA_HEADS, seq, MLA_QK_DIM), BF16),
            jax.ShapeDtypeStruct((batch, MLA_HEADS, seq, MLA_VP), BF16),
        ],
        compiler_params=_params("parallel"),
        name="mla_prep",
    )(c_q, c_kv, k_r2, pos_col, inv128, g_cq, g_ckv, wq_n, wq_r, wq_s, w_kv)


MLA_TILE = 512
MLA_HP = 2


def _mla_flash_kernel(q_ref, k_ref, v_ref, o_ref, s_sc, p_sc, alpha_sc, m_sc, acc_sc):
    qi = pl.program_id(2)
    t = q_ref.shape[2]
    heads = range(MLA_HP)

    def scores(hh, j):
        r0 = pl.multiple_of(j * t, t)
        return _dot_nt(q_ref[0, hh], k_ref[0, hh, pl.ds(r0, t), :])

    def softmax(hh, s, slot):
        m_prev = m_sc[hh]
        m_new = jnp.maximum(m_prev, jnp.max(s, axis=1, keepdims=True))
        alpha_sc[slot, hh] = jnp.exp2(m_prev - m_new)
        p_sc[slot, hh] = jnp.exp2(s - jnp.concatenate([m_new] * (t // 128), axis=1)).astype(BF16)
        m_sc[hh] = m_new

    def accumulate(hh, j, slot):
        r0 = pl.multiple_of(j * t, t)
        pv = _dot(p_sc[slot, hh], v_ref[0, hh, pl.ds(r0, t), :])
        acc_sc[hh] = jnp.concatenate([alpha_sc[slot, hh]] * (MLA_VP // 128), axis=1) * acc_sc[hh] + pv

    m_sc[...] = jnp.full(m_sc.shape, NEG, F32)
    acc_sc[...] = jnp.zeros(acc_sc.shape, F32)
    p_sc[1] = jnp.zeros(p_sc.shape[1:], BF16)
    alpha_sc[1] = jnp.ones(alpha_sc.shape[1:], F32)
    for hh in heads:
        s_sc[0, hh] = scores(hh, 0)

    def stage(j, slot):
        for hh in heads:
            s_sc[1 - slot, hh] = scores(hh, j + 1)
            softmax(hh, s_sc[slot, hh], slot)
            accumulate(hh, jnp.maximum(j - 1, 0), 1 - slot)

    def body(i, carry):
        stage(2 * i, 0)
        stage(2 * i + 1, 1)
        return carry

    lax.fori_loop(0, qi // 2, body, 0)

    def finish(slot):
        row = lax.broadcasted_iota(jnp.int32, (t, t), 0)
        col = lax.broadcasted_iota(jnp.int32, (t, t), 1)
        for hh in heads:
            softmax(hh, jnp.where(col <= row, s_sc[slot, hh], NEG), slot)
            accumulate(hh, jnp.maximum(qi - 1, 0), 1 - slot)
        for hh in heads:
            accumulate(hh, qi, slot)
            acc = acc_sc[hh]
            o_ref[0, :, hh * MLA_V_DIM:(hh + 1) * MLA_V_DIM] = (
                acc[:, :MLA_V_DIM] / acc[:, MLA_V_DIM:MLA_V_DIM + 1]).astype(o_ref.dtype)

    @pl.when(qi % 2 == 0)
    def _():
        finish(0)

    @pl.when(qi % 2 == 1)
    def _():
        stage(qi - 1, 0)
        finish(1)


def _mla_flash(q, k, v):
    B, H, S, _ = q.shape
    t = MLA_TILE
    hp = MLA_HP
    return pl.pallas_call(
        _mla_flash_kernel,
        grid=(B, H // hp, S // t),
        in_specs=[
            pl.BlockSpec((1, hp, t, MLA_QK_DIM), lambda b, h, i: (b, h, i, 0)),
            pl.BlockSpec((1, hp, S, MLA_QK_DIM), lambda b, h, i: (b, h, 0, 0)),
            pl.BlockSpec((1, hp, S, MLA_VP), lambda b, h, i: (b, h, 0, 0)),
        ],
        out_specs=pl.BlockSpec((1, t, hp * MLA_V_DIM), lambda b, h, i: (b, i, h)),
        out_shape=jax.ShapeDtypeStruct((B, S, MLA_V_W), BF16),
        scratch_shapes=[
            pltpu.VMEM((2, hp, t, t), F32),
            pltpu.VMEM((2, hp, t, t), BF16),
            pltpu.VMEM((2, hp, t, 128), F32),
            pltpu.VMEM((hp, t, 128), F32),
            pltpu.VMEM((hp, t, MLA_VP), F32),
        ],
        compiler_params=_params("parallel", "parallel", "arbitrary"),
        name="mla_flash",
    )(q, k, v)


def _mix_kernel(x_ref, g_ref, ya_in_ref, yb_in_ref, wga_ref, wgb_ref, wa_ref, wb_ref, wo_ref, o_ref, h_sc, acc_sc):
    n = pl.program_id(1)

    @pl.when(n == 0)
    def _():
        h_sc[...] = _rms(x_ref[...], g_ref[...]).astype(BF16)
        acc_sc[...] = jnp.zeros(acc_sc.shape, F32)

    h = h_sc[...]
    ga = _dot(h, wga_ref[...])
    gb = _dot(h, wgb_ref[...])
    ya = _dot(ya_in_ref[...], wa_ref[...])
    yb = _dot(yb_in_ref[...], wb_ref[...])
    mixed = jax.nn.sigmoid(ga) * ya + jax.nn.sigmoid(gb) * yb
    acc_sc[...] += _dot(mixed.astype(BF16), wo_ref[...])

    @pl.when(n == pl.num_programs(1) - 1)
    def _():
        o_ref[...] = x_ref[...] + acc_sc[...]


def _mix(x2, g, swa_out, mla_out, w_ga, w_gb, w_a, w_b, w_o, tm=512, tn=512):
    T, D = x2.shape
    return pl.pallas_call(
        _mix_kernel,
        grid=(T // tm, D // tn),
        in_specs=[
            pl.BlockSpec((tm, D), lambda i, n: (i, 0)),
            pl.BlockSpec((1, D), lambda i, n: (0, 0)),
            pl.BlockSpec((tm, SWA_Q_W), lambda i, n: (i, 0)),
            pl.BlockSpec((tm, MLA_V_W), lambda i, n: (i, 0)),
            pl.BlockSpec((D, tn), lambda i, n: (0, n)),
            pl.BlockSpec((D, tn), lambda i, n: (0, n)),
            pl.BlockSpec((SWA_Q_W, tn), lambda i, n: (0, n)),
            pl.BlockSpec((MLA_V_W, tn), lambda i, n: (0, n)),
            pl.BlockSpec((tn, D), lambda i, n: (n, 0)),
        ],
        out_specs=pl.BlockSpec((tm, D), lambda i, n: (i, 0)),
        out_shape=jax.ShapeDtypeStruct((T, D), F32),
        scratch_shapes=[pltpu.VMEM((tm, D), BF16), pltpu.VMEM((tm, D), F32)],
        compiler_params=_params("parallel", "arbitrary"),
        name="mix",
    )(x2, g, swa_out, mla_out, w_ga, w_gb, w_a, w_b, w_o)


PEER_LANES = 128
CAND_ROWS = 16 + 7 * 8 + 8


def _extract_top(s, iters, exact):
    n_rows = s.shape[0]
    rows = lax.broadcasted_iota(jnp.int32, s.shape, 0)
    rank = jnp.full(s.shape, float(iters), F32)
    vals = []
    for r in range(iters):
        m = jnp.max(s, axis=0, keepdims=True)
        if exact:
            idx = jnp.min(jnp.where(s == m, rows, n_rows), axis=0, keepdims=True)
            sel = rows == idx
        else:
            sel = s == m
        rank = jnp.where(sel, float(r), rank)
        s = jnp.where(sel, -jnp.inf, s)
        vals.append(m)
    return rank, vals


def _ranked_excess(rank, iters):
    return jnp.sum(jnp.where(rank < float(iters), 1.0, 0.0), axis=0, keepdims=True) - float(iters)


def _peer_route_kernel(x_ref, g_ref, wq_ref, keys_ref, h_out, r2_out, n1_out, e1_out, e2_out, q_sc):
    tm = x_ref.shape[0]
    h = _rms(x_ref[...], g_ref[...]).astype(BF16)
    h_out[...] = h
    q_sc[...] = _dot(h, wq_ref[...]).astype(BF16)

    def route(hd, t0, exact):
        c0 = pl.multiple_of(hd * 2 * PEER_HALF, 2 * PEER_HALF)
        q1 = q_sc[t0:t0 + PEER_LANES, pl.ds(c0, PEER_HALF)]
        q2 = q_sc[t0:t0 + PEER_LANES, pl.ds(c0 + PEER_HALF, PEER_HALF)]
        s1 = _dot_nt(keys_ref[2 * hd], q1)
        s2 = _dot_nt(keys_ref[2 * hd + 1], q2)
        r1, v1 = _extract_top(s1, PEER_TOPK, exact)
        r2, v2 = _extract_top(s2, PEER_TOPK, exact)
        v2_lo = jnp.concatenate(v2[:8], axis=0)
        v2_all = jnp.concatenate(v2, axis=0)
        slabs = [v1[0] + v2_all]
        for a in range(1, 8):
            slabs.append(v1[a] + v2_lo)
        slabs.append(jnp.concatenate(v1[8:], axis=0) + v2[0])
        cand = jnp.concatenate(slabs, axis=0)
        rc, vc = _extract_top(cand, PEER_TOPK, exact)
        picked = jnp.where(rc < float(PEER_TOPK), 1.0, 0.0)
        z = jnp.ones_like(vc[0])
        for r in range(1, PEER_TOPK):
            z = z + jnp.exp(vc[r] - vc[0])
        cnt_lo = picked[0:8]
        for a in range(1, 8):
            cnt_lo = cnt_lo + picked[8 + 8 * a:16 + 8 * a]
        first_col = jnp.sum(picked[72:80], axis=0, keepdims=True)
        cnt_hi = jnp.sum(picked[8:16], axis=0, keepdims=True)
        n1 = jnp.where(r1 == 0.0, cnt_hi, 0.0)
        for b in range(8):
            cnt_b = cnt_lo[b:b + 1] + first_col if b == 0 else cnt_lo[b:b + 1]
            n1 = n1 + jnp.where(r1 < cnt_b, 1.0, 0.0)
        r2_out[hd, :, t0:t0 + PEER_LANES] = r2.astype(r2_out.dtype)
        n1_out[hd, :, t0:t0 + PEER_LANES] = n1
        e1_out[hd, :, t0:t0 + PEER_LANES] = jnp.exp(s1 - v1[0])
        e2_out[hd, :, t0:t0 + PEER_LANES] = (jnp.exp(s2 - v2[0]) / z).astype(e2_out.dtype)
        if exact:
            return None
        excess = (_ranked_excess(r1, PEER_TOPK) + _ranked_excess(r2, PEER_TOPK)
                  + _ranked_excess(rc, PEER_TOPK))
        return jnp.max(excess)

    def head_body(hd, carry):
        starts = [ch * PEER_LANES for ch in range(tm // PEER_LANES)]
        ties = [route(hd, t0, exact=False) for t0 in starts]

        @pl.when(functools.reduce(jnp.maximum, ties) > 0.0)
        def _():
            for t0 in starts:
                route(hd, t0, exact=True)
        return carry

    lax.fori_loop(0, PEER_HEADS, head_body, 0)


def _peer_route(x1, g, w_q, keys, tm=512):
    T, D = x1.shape
    side = jax.ShapeDtypeStruct((PEER_HEADS, PEER_N_KEYS, T), F32)
    side_bf = jax.ShapeDtypeStruct((PEER_HEADS, PEER_N_KEYS, T), BF16)
    side_spec = pl.BlockSpec((PEER_HEADS, PEER_N_KEYS, tm), lambda i: (0, 0, i))
    return pl.pallas_call(
        _peer_route_kernel,
        grid=(T // tm,),
        in_specs=[
            pl.BlockSpec((tm, D), lambda i: (i, 0)),
            pl.BlockSpec((1, D), lambda i: (0, 0)),
            pl.BlockSpec(w_q.shape, lambda i: (0, 0)),
            pl.BlockSpec(keys.shape, lambda i: (0, 0, 0)),
        ],
        out_specs=[pl.BlockSpec((tm, D), lambda i: (i, 0)), side_spec, side_spec, side_spec, side_spec],
        out_shape=[jax.ShapeDtypeStruct((T, D), BF16), side_bf, side, side, side_bf],
        scratch_shapes=[pltpu.VMEM((tm, w_q.shape[1]), BF16)],
        compiler_params=_params("parallel"),
        name="peer_route",
    )(x1, g, w_q, keys)


def _peer_dense_kernel(h_ref, u_ref, vt_ref, r2_ref, n1_ref, e1_ref, e2_ref, o_ref, acc_sc):
    e = pl.program_id(1)
    te = u_ref.shape[0]

    @pl.when(e == 0)
    def _():
        acc_sc[...] = jnp.zeros(acc_sc.shape, F32)

    a_t = _dot_nt(u_ref[...].astype(BF16), h_ref[...])
    act = (0.5 * a_t * (1.0 + lax.erf(a_t * math.sqrt(0.5)))).astype(BF16)
    zero = jnp.zeros((), BF16)
    w_rows = []
    for ii in range(te // PEER_N_KEYS):
        i = e * (te // PEER_N_KEYS) + ii
        gate = None
        for hd in range(PEER_HEADS):
            n1 = n1_ref[hd, pl.ds(i, 1), :].astype(BF16)
            e1 = e1_ref[hd, pl.ds(i, 1), :].astype(BF16)
            term = jnp.where(r2_ref[hd] < n1, e2_ref[hd] * e1, zero)
            gate = term if gate is None else gate + term
        w_rows.append(act[ii * PEER_N_KEYS:(ii + 1) * PEER_N_KEYS] * gate)
    w = jnp.concatenate(w_rows, axis=0) if len(w_rows) > 1 else w_rows[0]
    acc_sc[...] += _dot(vt_ref[...].T, w)

    @pl.when(e == pl.num_programs(1) - 1)
    def _():
        o_ref[...] = acc_sc[...].T.astype(o_ref.dtype)


def _peer_dense(h2, u_bf, vt_bf, r2, n1, e1, e2, tm=1024, te=512):
    T, D = h2.shape
    once = pl.Buffered(1)
    side = (PEER_HEADS, PEER_N_KEYS, tm)
    side_idx = lambda i, e: (0, 0, i)
    return pl.pallas_call(
        _peer_dense_kernel,
        grid=(T // tm, PEER_N_EXPERTS // te),
        in_specs=[
            pl.BlockSpec((tm, D), lambda i, e: (i, 0)),
            pl.BlockSpec((te, D), lambda i, e: (e, 0)),
            pl.BlockSpec((te, D), lambda i, e: (e, 0)),
            pl.BlockSpec(side, side_idx),
            pl.BlockSpec(side, side_idx, pipeline_mode=once),
            pl.BlockSpec(side, side_idx, pipeline_mode=once),
            pl.BlockSpec(side, side_idx),
        ],
        out_specs=pl.BlockSpec((tm, D), lambda i, e: (i, 0), pipeline_mode=once),
        out_shape=jax.ShapeDtypeStruct((T, D), BF16),
        scratch_shapes=[pltpu.VMEM((D, tm), F32)],
        compiler_params=_params("parallel", "arbitrary"),
        name="peer_dense",
    )(h2, u_bf, vt_bf, r2, n1, e1, e2)


def _final_kernel(x_ref, p_ref, g_ref, o_ref):
    o_ref[...] = _rms(x_ref[...] + p_ref[...].astype(F32), g_ref[...])


def _final_norm(x1, peer_out, g, tm=512):
    T, D = x1.shape
    return pl.pallas_call(
        _final_kernel,
        grid=(T // tm,),
        in_specs=[
            pl.BlockSpec((tm, D), lambda i: (i, 0)),
            pl.BlockSpec((tm, D), lambda i: (i, 0)),
            pl.BlockSpec((1, D), lambda i: (0, 0)),
        ],
        out_specs=pl.BlockSpec((tm, D), lambda i: (i, 0)),
        out_shape=jax.ShapeDtypeStruct((T, D), F32),
        compiler_params=_params("parallel"),
        name="final_norm",
    )(x1, peer_out, g)


def _layer(x2, pos_col, pos_row, batch, seq, g_mix, w_in, sinks, g_cq, w_uq, g_ckv, w_ukv, w_a_proj, w_b_proj,
           w_o, g_ffn, w_peer_q, peer_keys, peer_u, peer_v):
    D = x2.shape[1]
    o = np.cumsum([0, SWA_Q_W, SWA_KV_W, SWA_KV_W, MLA_Q_RANK, MLA_KV_RANK, MLA_ROPE_DIM, D, D])
    w_ga, w_gb = w_in[:, o[6]:o[7]], w_in[:, o[7]:o[8]]
    half = MLA_ROPE_DIM // 2

    q_h, c_q, k_h, v_h, c_kv, k_r2 = _attn_inproj(x2, g_mix.reshape(1, D), w_in)
    swa_out = _swa(q_h, k_h, v_h, pos_col, pos_row, sinks.astype(F32), seq)

    w_uq3 = w_uq.reshape(MLA_Q_RANK, MLA_HEADS, MLA_QK_DIM)
    wq_n = w_uq3[:, :, :MLA_NOPE_DIM].reshape(MLA_Q_RANK, MLA_HEADS * MLA_NOPE_DIM).astype(BF16)
    t1 = w_uq3[:, :, MLA_NOPE_DIM:MLA_NOPE_DIM + half]
    t2 = w_uq3[:, :, MLA_NOPE_DIM + half:]
    wq_r = jnp.concatenate([t1, t2], axis=2).reshape(MLA_Q_RANK, MLA_HEADS * MLA_ROPE_DIM).astype(BF16)
    wq_s = jnp.concatenate([-t2, t1], axis=2).reshape(MLA_Q_RANK, MLA_HEADS * MLA_ROPE_DIM).astype(BF16)
    inv = 1.0 / (ROPE_THETA ** (jnp.arange(0, MLA_ROPE_DIM, 2, dtype=F32) / MLA_ROPE_DIM))
    inv128 = jnp.tile(inv, 128 // half).reshape(1, 128)
    q, k, v = _mla_prep(c_q, c_kv, k_r2, pos_col, inv128, g_cq.reshape(1, -1), g_ckv.reshape(1, -1),
                        wq_n, wq_r, wq_s, w_ukv.astype(BF16), batch, seq)
    mla_out = _mla_flash(q, k, v).reshape(batch * seq, MLA_V_W)

    x1 = _mix(x2, g_mix.reshape(1, D), swa_out, mla_out, w_ga.astype(BF16), w_gb.astype(BF16),
              w_a_proj.astype(BF16), w_b_proj.astype(BF16), w_o.astype(BF16))

    keys = peer_keys.reshape(PEER_HEADS * 2, PEER_N_KEYS, PEER_HALF).astype(BF16)
    h2, r2, n1, e1, e2 = _peer_route(x1, g_ffn.reshape(1, D), w_peer_q.astype(BF16), keys)
    peer_out = _peer_dense(h2, peer_u, peer_v.astype(BF16), r2, n1, e1, e2)
    return x1, peer_out


def kernel(x, positions, g_mix, w_in, sinks, g_cq, w_uq, g_ckv, w_ukv, w_a_proj, w_b_proj, w_o, g_ffn, w_peer_q,
           peer_keys, peer_u, peer_v, g_final):
    batch, seq, D = x.shape
    depth = g_mix.shape[0]
    assert seq % 1024 == 0 and D % 512 == 0, (seq, D)
    x2 = x.reshape(batch * seq, D)
    pos_f = positions.astype(F32)
    pos_col = pos_f.reshape(batch * seq, 1)
    pos_row = pos_f.reshape(1, batch * seq)
    peer_out = None
    for l in range(depth):
        if peer_out is not None:
            x2 = x2 + peer_out
        x2, peer_out = _layer(x2, pos_col, pos_row, batch, seq, g_mix[l], w_in[l], sinks[l], g_cq[l], w_uq[l],
                              g_ckv[l], w_ukv[l], w_a_proj[l], w_b_proj[l], w_o[l], g_ffn[l], w_peer_q[l],
                              peer_keys[l], peer_u[l], peer_v[l])
    out = _final_norm(x2, peer_out, g_final.reshape(1, D))
    return out.reshape(batch, seq, D)
```

```python
import functools
import math

import numpy as np
import jax
import jax.numpy as jnp
from jax import lax
from jax.experimental import pallas as pl
from jax.experimental.pallas import tpu as pltpu

F32 = jnp.float32
BF16 = jnp.bfloat16

EPS = 1e-6
NEG = -1e30
BLOCK = 128
SWA_HEADS = 16
SWA_KV_HEADS = 4
SWA_GROUP = SWA_HEADS // SWA_KV_HEADS
SWA_HEAD_DIM = 64
SWA_WINDOW = 128
SWA_Q_W = SWA_HEADS * SWA_HEAD_DIM
SWA_KV_W = SWA_KV_HEADS * SWA_HEAD_DIM
MLA_HEADS = 8
MLA_Q_RANK = 512
MLA_KV_RANK = 256
MLA_NOPE_DIM = 128
MLA_ROPE_DIM = 64
MLA_QK_DIM = MLA_NOPE_DIM + MLA_ROPE_DIM
MLA_V_DIM = 128
MLA_V_W = MLA_HEADS * MLA_V_DIM
ROPE_THETA = 10000.0
PEER_HEADS = 8
PEER_N_KEYS = 128
PEER_N_EXPERTS = PEER_N_KEYS * PEER_N_KEYS
PEER_HALF = 128
PEER_TOPK = 16

V7X_VMEM_BYTES = 64 * 1024 * 1024
VMEM_LIMIT = V7X_VMEM_BYTES - 8 * 1024 * 1024


def _params(*semantics):
    return pltpu.CompilerParams(dimension_semantics=semantics, vmem_limit_bytes=VMEM_LIMIT)


def _rms(xf, g):
    y = xf * lax.rsqrt(jnp.mean(xf * xf, axis=-1, keepdims=True) + EPS)
    return y * g


def _dot(a, b):
    return jnp.dot(a, b, preferred_element_type=F32)


def _dot_nt(a, b):
    return lax.dot_general(a, b, (((1,), (1,)), ((), ())), preferred_element_type=F32)


ATTN_COLS = SWA_Q_W + 2 * SWA_KV_W + MLA_Q_RANK + MLA_KV_RANK + MLA_ROPE_DIM
ATTN_WIN = -(-ATTN_COLS // 128) * 128
LOG2E = math.log2(math.e)


def _attn_inproj_kernel(x_ref, g_ref, w_ref, q_out, cq_out, k_out, v_out, ckv_out, kr_out, w_sc):
    @pl.when(pl.program_id(0) == 0)
    def _():
        w_sc[...] = w_ref[...].astype(BF16)

    h = _rms(x_ref[...], g_ref[...]).astype(BF16)
    z = _dot_nt(h, w_sc[...])
    d = SWA_HEAD_DIM
    off = 0
    q = (z[:, :SWA_Q_W] * (d ** -0.5 * LOG2E)).astype(BF16)
    for hd in range(SWA_HEADS):
        q_out[hd] = q[:, hd * d:(hd + 1) * d]
    off += SWA_Q_W
    for hd in range(SWA_KV_HEADS):
        k_out[hd] = z[:, off + hd * d:off + (hd + 1) * d].astype(BF16)
    off += SWA_KV_W
    for hd in range(SWA_KV_HEADS):
        v_out[hd] = z[:, off + hd * d:off + (hd + 1) * d].astype(BF16)
    off += SWA_KV_W
    cq_out[...] = z[:, off:off + MLA_Q_RANK].astype(BF16)
    off += MLA_Q_RANK
    ckv_out[...] = z[:, off:off + MLA_KV_RANK].astype(BF16)
    off += MLA_KV_RANK
    half = MLA_ROPE_DIM // 2
    t1 = z[:, off:off + half]
    t2 = z[:, off + half:off + MLA_ROPE_DIM]
    kr_out[...] = jnp.concatenate([t1, t2, -t2, t1], axis=1).astype(BF16)


def _attn_inproj(x2, g, w_in, tm=512):
    T, D = x2.shape
    d = SWA_HEAD_DIM
    row = lambda i: (i, 0)
    head_major = lambda i: (0, i, 0)
    return pl.pallas_call(
        _attn_inproj_kernel,
        grid=(T // tm,),
        in_specs=[
            pl.BlockSpec((tm, D), row),
            pl.BlockSpec((1, D), lambda i: (0, 0)),
            pl.BlockSpec((ATTN_WIN, D), lambda i: (0, 0), pipeline_mode=pl.Buffered(1)),
        ],
        out_specs=[
            pl.BlockSpec((SWA_HEADS, tm, d), head_major),
            pl.BlockSpec((tm, MLA_Q_RANK), row),
            pl.BlockSpec((SWA_KV_HEADS, tm, d), head_major),
            pl.BlockSpec((SWA_KV_HEADS, tm, d), head_major),
            pl.BlockSpec((tm, MLA_KV_RANK), row),
            pl.BlockSpec((tm, 2 * MLA_ROPE_DIM), row),
        ],
        out_shape=[
            jax.ShapeDtypeStruct((SWA_HEADS, T, d), BF16),
            jax.ShapeDtypeStruct((T, MLA_Q_RANK), BF16),
            jax.ShapeDtypeStruct((SWA_KV_HEADS, T, d), BF16),
            jax.ShapeDtypeStruct((SWA_KV_HEADS, T, d), BF16),
            jax.ShapeDtypeStruct((T, MLA_KV_RANK), BF16),
            jax.ShapeDtypeStruct((T, 2 * MLA_ROPE_DIM), BF16),
        ],
        scratch_shapes=[pltpu.VMEM((ATTN_WIN, D), BF16)],
        compiler_params=_params("arbitrary"),
        name="attn_inproj",
    )(x2, g, w_in)


SWA_TILE = 512
SWA_SUB = SWA_TILE // BLOCK


def _swa_kernel(tiles_per_seq, sinks_ref, q_ref, kc_ref, kp_ref, vc_ref, vp_ref, pc_ref, prc_ref, prp_ref,
                o_ref, kbuf, vbuf, pkbuf):
    i = pl.program_id(0)
    kbuf[:, 0:BLOCK, :] = kp_ref[...]
    kbuf[:, BLOCK:, :] = kc_ref[...]
    vbuf[:, 0:BLOCK, :] = vp_ref[...]
    vbuf[:, BLOCK:, :] = vc_ref[...]
    pkbuf[:, 0:BLOCK] = prp_ref[...]
    pkbuf[:, BLOCK:] = prc_ref[...]
    first_tile = (i % tiles_per_seq) == 0

    qi = lax.broadcasted_iota(jnp.int32, (BLOCK, 2 * BLOCK), 0)
    kj = lax.broadcasted_iota(jnp.int32, (BLOCK, 2 * BLOCK), 1)
    rel = BLOCK + qi - kj
    band = (rel >= 0) & (rel < SWA_WINDOW)
    d = SWA_HEAD_DIM
    ones_rhs = jnp.ones((2 * BLOCK, 128), BF16)

    for c in range(SWA_SUB):
        r0 = c * BLOCK
        pq = pc_ref[r0:r0 + BLOCK, :]
        pk = pkbuf[:, r0:r0 + 2 * BLOCK]
        dist = jnp.abs(pq - pk) * LOG2E
        if c == 0:
            mask = band & (jnp.logical_not(first_tile) | (kj >= BLOCK))
        else:
            mask = band
        for g in range(SWA_KV_HEADS):
            qs = jnp.concatenate([q_ref[g * SWA_GROUP + hh, r0:r0 + BLOCK, :] for hh in range(SWA_GROUP)], axis=0)
            logits = _dot_nt(qs, kbuf[g, r0:r0 + 2 * BLOCK, :])
            probs, denoms = [], []
            for hh in range(SWA_GROUP):
                h = g * SWA_GROUP + hh
                slope = 2.0 ** (-8.0 * (h + 1) / SWA_HEADS)
                lh = logits[hh * BLOCK:(hh + 1) * BLOCK] - slope * dist
                lh = jnp.where(mask, lh, NEG)
                sink = sinks_ref[h] * LOG2E
                m = jnp.maximum(jnp.max(lh, axis=-1, keepdims=True), sink)
                p = jnp.exp2(lh - m)
                denoms.append(jnp.exp2(sink - m))
                probs.append(p.astype(BF16))
            pm = jnp.concatenate(probs, axis=0)
            o = _dot(pm, vbuf[g, r0:r0 + 2 * BLOCK, :])
            row_sum = _dot(pm, ones_rhs)
            for hh in range(SWA_GROUP):
                h = g * SWA_GROUP + hh
                denom = row_sum[hh * BLOCK:(hh + 1) * BLOCK, :d] + denoms[hh]
                o_ref[r0:r0 + BLOCK, h * d:(h + 1) * d] = (o[hh * BLOCK:(hh + 1) * BLOCK] / denom).astype(o_ref.dtype)


def _swa(q_h, k_h, v_h, pos_col, pos_row, sinks, seq):
    T = q_h.shape[1]
    d = SWA_HEAD_DIM
    tiles_per_seq = seq // SWA_TILE

    def prev_blk(i):
        return jnp.where(i % tiles_per_seq == 0, i * SWA_SUB, i * SWA_SUB - 1)

    cur = lambda i: (0, i, 0)
    prev = lambda i: (0, prev_blk(i), 0)
    return pl.pallas_call(
        functools.partial(_swa_kernel, tiles_per_seq),
        grid=(T // SWA_TILE,),
        in_specs=[
            pl.BlockSpec(memory_space=pltpu.SMEM),
            pl.BlockSpec((SWA_HEADS, SWA_TILE, d), cur),
            pl.BlockSpec((SWA_KV_HEADS, SWA_TILE, d), cur),
            pl.BlockSpec((SWA_KV_HEADS, BLOCK, d), prev),
            pl.BlockSpec((SWA_KV_HEADS, SWA_TILE, d), cur),
            pl.BlockSpec((SWA_KV_HEADS, BLOCK, d), prev),
            pl.BlockSpec((SWA_TILE, 1), lambda i: (i, 0)),
            pl.BlockSpec((1, SWA_TILE), lambda i: (0, i)),
            pl.BlockSpec((1, BLOCK), lambda i: (0, prev_blk(i))),
        ],
        out_specs=pl.BlockSpec((SWA_TILE, SWA_Q_W), lambda i: (i, 0)),
        out_shape=jax.ShapeDtypeStruct((T, SWA_Q_W), BF16),
        scratch_shapes=[
            pltpu.VMEM((SWA_KV_HEADS, SWA_TILE + BLOCK, d), BF16),
            pltpu.VMEM((SWA_KV_HEADS, SWA_TILE + BLOCK, d), BF16),
            pltpu.VMEM((1, SWA_TILE + BLOCK), F32),
        ],
        compiler_params=_params("parallel"),
        name="swa",
    )(sinks, q_h, k_h, k_h, v_h, v_h, pos_col, pos_row, pos_row)


MLA_VP = 2 * MLA_V_DIM


def _mla_prep_kernel(cq_ref, ckv_ref, kr_ref, pos_ref, inv_ref, gcq_ref, gckv_ref, wqn_ref, wqr_ref, wqs_ref,
                     wkv_ref, q_out, k_out, v_out):
    scale = MLA_QK_DIM ** -0.5 * math.log2(math.e)
    ang = pos_ref[...] * inv_ref[...]
    cos = jnp.cos(ang)
    sin = jnp.sin(ang)
    cos4 = jnp.concatenate([cos] * 4, axis=1)
    sin4 = jnp.concatenate([sin] * 4, axis=1)

    cn = _rms(cq_ref[...].astype(F32), gcq_ref[...]).astype(BF16)
    qn = _dot(cn, wqn_ref[...])
    qr = _dot(cn, wqr_ref[...])
    qs = _dot(cn, wqs_ref[...])
    qrot = qr * cos4 + qs * sin4

    kvn = _rms(ckv_ref[...].astype(F32), gckv_ref[...]).astype(BF16)
    kv = _dot(kvn, wkv_ref[...])
    kr = kr_ref[...].astype(F32)
    krot = (kr[:, :MLA_ROPE_DIM] * cos[:, :MLA_ROPE_DIM]
            + kr[:, MLA_ROPE_DIM:] * sin[:, :MLA_ROPE_DIM]).astype(BF16)

    lane = lax.broadcasted_iota(jnp.int32, (kr.shape[0], MLA_VP - MLA_V_DIM), 1)
    ones_col = jnp.where(lane == 0, 1.0, 0.0).astype(BF16)
    for h in range(MLA_HEADS):
        q_h = jnp.concatenate(
            [qn[:, h * MLA_NOPE_DIM:(h + 1) * MLA_NOPE_DIM], qrot[:, h * MLA_ROPE_DIM:(h + 1) * MLA_ROPE_DIM]],
            axis=1) * scale
        q_out[0, h] = q_h.astype(BF16)
        kn_h = kv[:, h * 256:h * 256 + MLA_NOPE_DIM].astype(BF16)
        k_out[0, h] = jnp.concatenate([kn_h, krot], axis=1)
        v_out[0, h, :, :MLA_V_DIM] = kv[:, h * 256 + MLA_NOPE_DIM:(h + 1) * 256].astype(BF16)
        v_out[0, h, :, MLA_V_DIM:] = ones_col


def _mla_prep(c_q, c_kv, k_r2, pos_col, inv128, g_cq, g_ckv, wq_n, wq_r, wq_s, w_kv, batch, seq, tm=512):
    T = c_q.shape[0]
    tps = seq // tm
    const = lambda i: (0, 0)
    out_idx = lambda i: (i // tps, 0, i % tps, 0)
    return pl.pallas_call(
        _mla_prep_kernel,
        grid=(T // tm,),
        in_specs=[
            pl.BlockSpec((tm, MLA_Q_RANK), lambda i: (i, 0)),
            pl.BlockSpec((tm, MLA_KV_RANK), lambda i: (i, 0)),
            pl.BlockSpec((tm, 2 * MLA_ROPE_DIM), lambda i: (i, 0)),
            pl.BlockSpec((tm, 1), lambda i: (i, 0)),
            pl.BlockSpec((1, 128), const),
            pl.BlockSpec((1, MLA_Q_RANK), const),
            pl.BlockSpec((1, MLA_KV_RANK), const),
            pl.BlockSpec(wq_n.shape, const),
            pl.BlockSpec(wq_r.shape, const),
            pl.BlockSpec(wq_s.shape, const),
            pl.BlockSpec(w_kv.shape, const),
        ],
        out_specs=[
            pl.BlockSpec((1, MLA_HEADS, tm, MLA_QK_DIM), out_idx),
            pl.BlockSpec((1, MLA_HEADS, tm, MLA_QK_DIM), out_idx),
            pl.BlockSpec((1, MLA_HEADS, tm, MLA_VP), out_idx),
        ],
        out_shape=[
            jax.ShapeDtypeStruct((batch, MLA_HEADS, seq, MLA_QK_DIM), BF16),
            jax.ShapeDtypeStruct((batch, MLA_HEADS, seq, MLA_QK_DIM), BF16),
            jax.ShapeDtypeStruct((batch, MLA_HEADS, seq, MLA_VP), BF16),
        ],
        compiler_params=_params("parallel"),
        name="mla_prep",
    )(c_q, c_kv, k_r2, pos_col, inv128, g_cq, g_ckv, wq_n, wq_r, wq_s, w_kv)


MLA_TILE = 512
MLA_HP = 2


def _mla_flash_kernel(q_ref, k_ref, v_ref, o_ref, s_sc, p_sc, alpha_sc, m_sc, acc_sc):
    qi = pl.program_id(2)
    t = q_ref.shape[2]
    heads = range(MLA_HP)

    def scores(hh, j):
        r0 = pl.multiple_of(j * t, t)
        return _dot_nt(q_ref[0, hh], k_ref[0, hh, pl.ds(r0, t), :])

    def softmax(hh, s, slot):
        m_prev = m_sc[hh]
        m_new = jnp.maximum(m_prev, jnp.max(s, axis=1, keepdims=True))
        alpha_sc[slot, hh] = jnp.exp2(m_prev - m_new)
        p_sc[slot, hh] = jnp.exp2(s - jnp.concatenate([m_new] * (t // 128), axis=1)).astype(BF16)
        m_sc[hh] = m_new

    def accumulate(hh, j, slot):
        r0 = pl.multiple_of(j * t, t)
        pv = _dot(p_sc[slot, hh], v_ref[0, hh, pl.ds(r0, t), :])
        acc_sc[hh] = jnp.concatenate([alpha_sc[slot, hh]] * (MLA_VP // 128), axis=1) * acc_sc[hh] + pv

    m_sc[...] = jnp.full(m_sc.shape, NEG, F32)
    acc_sc[...] = jnp.zeros(acc_sc.shape, F32)
    p_sc[1] = jnp.zeros(p_sc.shape[1:], BF16)
    alpha_sc[1] = jnp.ones(alpha_sc.shape[1:], F32)
    for hh in heads:
        s_sc[0, hh] = scores(hh, 0)

    def stage(j, slot):
        for hh in heads:
            s_sc[1 - slot, hh] = scores(hh, j + 1)
            softmax(hh, s_sc[slot, hh], slot)
            accumulate(hh, jnp.maximum(j - 1, 0), 1 - slot)

    def body(i, carry):
        stage(2 * i, 0)
        stage(2 * i + 1, 1)
        return carry

    lax.fori_loop(0, qi // 2, body, 0)

    def finish(slot):
        row = lax.broadcasted_iota(jnp.int32, (t, t), 0)
        col = lax.broadcasted_iota(jnp.int32, (t, t), 1)
        for hh in heads:
            softmax(hh, jnp.where(col <= row, s_sc[slot, hh], NEG), slot)
            accumulate(hh, jnp.maximum(qi - 1, 0), 1 - slot)
        for hh in heads:
            accumulate(hh, qi, slot)
            acc = acc_sc[hh]
            o_ref[0, :, hh * MLA_V_DIM:(hh + 1) * MLA_V_DIM] = (
                acc[:, :MLA_V_DIM] / acc[:, MLA_V_DIM:MLA_V_DIM + 1]).astype(o_ref.dtype)

    @pl.when(qi % 2 == 0)
    def _():
        finish(0)

    @pl.when(qi % 2 == 1)
    def _():
        stage(qi - 1, 0)
        finish(1)


def _mla_flash(q, k, v):
    B, H, S, _ = q.shape
    t = MLA_TILE
    hp = MLA_HP
    return pl.pallas_call(
        _mla_flash_kernel,
        grid=(B, H // hp, S // t),
        in_specs=[
            pl.BlockSpec((1, hp, t, MLA_QK_DIM), lambda b, h, i: (b, h, i, 0)),
            pl.BlockSpec((1, hp, S, MLA_QK_DIM), lambda b, h, i: (b, h, 0, 0)),
            pl.BlockSpec((1, hp, S, MLA_VP), lambda b, h, i: (b, h, 0, 0)),
        ],
        out_specs=pl.BlockSpec((1, t, hp * MLA_V_DIM), lambda b, h, i: (b, i, h)),
        out_shape=jax.ShapeDtypeStruct((B, S, MLA_V_W), BF16),
        scratch_shapes=[
            pltpu.VMEM((2, hp, t, t), F32),
            pltpu.VMEM((2, hp, t, t), BF16),
            pltpu.VMEM((2, hp, t, 128), F32),
            pltpu.VMEM((hp, t, 128), F32),
            pltpu.VMEM((hp, t, MLA_VP), F32),
        ],
        compiler_params=_params("parallel", "parallel", "arbitrary"),
        name="mla_flash",
    )(q, k, v)


def _mix_kernel(x_ref, g_ref, ya_in_ref, yb_in_ref, wga_ref, wgb_ref, wa_ref, wb_ref, wo_ref, o_ref, h_sc, acc_sc):
    n = pl.program_id(1)

    @pl.when(n == 0)
    def _():
        h_sc[...] = _rms(x_ref[...], g_ref[...]).astype(BF16)
        acc_sc[...] = jnp.zeros(acc_sc.shape, F32)

    h = h_sc[...]
    ga = _dot_nt(h, wga_ref[...])
    gb = _dot_nt(h, wgb_ref[...])
    ya = _dot(ya_in_ref[...], wa_ref[...])
    yb = _dot(yb_in_ref[...], wb_ref[...])
    mixed = jax.nn.sigmoid(ga) * ya + jax.nn.sigmoid(gb) * yb
    acc_sc[...] += _dot(mixed.astype(BF16), wo_ref[...])

    @pl.when(n == pl.num_programs(1) - 1)
    def _():
        o_ref[...] = x_ref[...] + acc_sc[...]


def _mix(x2, g, swa_out, mla_out, w_ga, w_gb, w_a, w_b, w_o, tm=512, tn=512):
    T, D = x2.shape
    return pl.pallas_call(
        _mix_kernel,
        grid=(T // tm, D // tn),
        in_specs=[
            pl.BlockSpec((tm, D), lambda i, n: (i, 0)),
            pl.BlockSpec((1, D), lambda i, n: (0, 0)),
            pl.BlockSpec((tm, SWA_Q_W), lambda i, n: (i, 0)),
            pl.BlockSpec((tm, MLA_V_W), lambda i, n: (i, 0)),
            pl.BlockSpec((tn, D), lambda i, n: (n, 0)),
            pl.BlockSpec((tn, D), lambda i, n: (n + D // tn, 0)),
            pl.BlockSpec((SWA_Q_W, tn), lambda i, n: (0, n)),
            pl.BlockSpec((MLA_V_W, tn), lambda i, n: (0, n)),
            pl.BlockSpec((tn, D), lambda i, n: (n, 0)),
        ],
        out_specs=pl.BlockSpec((tm, D), lambda i, n: (i, 0)),
        out_shape=jax.ShapeDtypeStruct((T, D), F32),
        scratch_shapes=[pltpu.VMEM((tm, D), BF16), pltpu.VMEM((tm, D), F32)],
        compiler_params=_params("parallel", "arbitrary"),
        name="mix",
    )(x2, g, swa_out, mla_out, w_ga, w_gb, w_a, w_b, w_o)


PEER_LANES = 128
CAND_ROWS = 16 + 7 * 8 + 8


def _extract_top(s, iters, exact):
    n_rows = s.shape[0]
    rows = lax.broadcasted_iota(jnp.int32, s.shape, 0)
    rank = jnp.full(s.shape, float(iters), F32)
    vals = []
    for r in range(iters):
        m = jnp.max(s, axis=0, keepdims=True)
        if exact:
            idx = jnp.min(jnp.where(s == m, rows, n_rows), axis=0, keepdims=True)
            sel = rows == idx
        else:
            sel = s == m
        rank = jnp.where(sel, float(r), rank)
        s = jnp.where(sel, -jnp.inf, s)
        vals.append(m)
    return rank, vals


def _ranked_excess(rank, iters):
    return jnp.sum(jnp.where(rank < float(iters), 1.0, 0.0), axis=0, keepdims=True) - float(iters)


def _peer_route_kernel(x_ref, g_ref, wq_ref, keys_ref, h_out, r2_out, n1_out, e1_out, e2_out, q_sc):
    tm = x_ref.shape[0]
    h = _rms(x_ref[...], g_ref[...]).astype(BF16)
    h_out[...] = h
    q_sc[...] = _dot(h, wq_ref[...]).astype(BF16)

    def route(hd, t0, exact):
        c0 = pl.multiple_of(hd * 2 * PEER_HALF, 2 * PEER_HALF)
        q1 = q_sc[t0:t0 + PEER_LANES, pl.ds(c0, PEER_HALF)]
        q2 = q_sc[t0:t0 + PEER_LANES, pl.ds(c0 + PEER_HALF, PEER_HALF)]
        s1 = _dot_nt(keys_ref[2 * hd], q1)
        s2 = _dot_nt(keys_ref[2 * hd + 1], q2)
        r1, v1 = _extract_top(s1, PEER_TOPK, exact)
        r2, v2 = _extract_top(s2, PEER_TOPK, exact)
        v2_lo = jnp.concatenate(v2[:8], axis=0)
        v2_all = jnp.concatenate(v2, axis=0)
        slabs = [v1[0] + v2_all]
        for a in range(1, 8):
            slabs.append(v1[a] + v2_lo)
        slabs.append(jnp.concatenate(v1[8:], axis=0) + v2[0])
        cand = jnp.concatenate(slabs, axis=0)
        rc, vc = _extract_top(cand, PEER_TOPK, exact)
        picked = jnp.where(rc < float(PEER_TOPK), 1.0, 0.0)
        z = jnp.ones_like(vc[0])
        for r in range(1, PEER_TOPK):
            z = z + jnp.exp(vc[r] - vc[0])
        cnt_lo = picked[0:8]
        for a in range(1, 8):
            cnt_lo = cnt_lo + picked[8 + 8 * a:16 + 8 * a]
        first_col = jnp.sum(picked[72:80], axis=0, keepdims=True)
        cnt_hi = jnp.sum(picked[8:16], axis=0, keepdims=True)
        n1 = jnp.where(r1 == 0.0, cnt_hi, 0.0)
        for b in range(8):
            cnt_b = cnt_lo[b:b + 1] + first_col if b == 0 else cnt_lo[b:b + 1]
            n1 = n1 + jnp.where(r1 < cnt_b, 1.0, 0.0)
        r2_out[hd, :, t0:t0 + PEER_LANES] = r2.astype(r2_out.dtype)
        n1_out[hd, :, t0:t0 + PEER_LANES] = n1
        e1_out[hd, :, t0:t0 + PEER_LANES] = jnp.exp(s1 - v1[0])
        e2_out[hd, :, t0:t0 + PEER_LANES] = (jnp.exp(s2 - v2[0]) / z).astype(e2_out.dtype)
        if exact:
            return None
        excess = (_ranked_excess(r1, PEER_TOPK) + _ranked_excess(r2, PEER_TOPK)
                  + _ranked_excess(rc, PEER_TOPK))
        return jnp.max(excess)

    def head_body(hd, carry):
        starts = [ch * PEER_LANES for ch in range(tm // PEER_LANES)]
        ties = [route(hd, t0, exact=False) for t0 in starts]

        @pl.when(functools.reduce(jnp.maximum, ties) > 0.0)
        def _():
            for t0 in starts:
                route(hd, t0, exact=True)
        return carry

    lax.fori_loop(0, PEER_HEADS, head_body, 0)


def _peer_route(x1, g, w_q, keys, tm=512):
    T, D = x1.shape
    side = jax.ShapeDtypeStruct((PEER_HEADS, PEER_N_KEYS, T), F32)
    side_bf = jax.ShapeDtypeStruct((PEER_HEADS, PEER_N_KEYS, T), BF16)
    side_spec = pl.BlockSpec((PEER_HEADS, PEER_N_KEYS, tm), lambda i: (0, 0, i))
    return pl.pallas_call(
        _peer_route_kernel,
        grid=(T // tm,),
        in_specs=[
            pl.BlockSpec((tm, D), lambda i: (i, 0)),
            pl.BlockSpec((1, D), lambda i: (0, 0)),
            pl.BlockSpec(w_q.shape, lambda i: (0, 0)),
            pl.BlockSpec(keys.shape, lambda i: (0, 0, 0)),
        ],
        out_specs=[pl.BlockSpec((tm, D), lambda i: (i, 0)), side_spec, side_spec, side_spec, side_spec],
        out_shape=[jax.ShapeDtypeStruct((T, D), BF16), side_bf, side, side, side_bf],
        scratch_shapes=[pltpu.VMEM((tm, w_q.shape[1]), BF16)],
        compiler_params=_params("parallel"),
        name="peer_route",
    )(x1, g, w_q, keys)


def _peer_dense_kernel(h_ref, u_ref, vt_ref, r2_ref, n1_ref, e1_ref, e2_ref, o_ref, acc_sc):
    e = pl.program_id(1)
    te = u_ref.shape[0]

    @pl.when(e == 0)
    def _():
        acc_sc[...] = jnp.zeros(acc_sc.shape, F32)

    a_t = _dot_nt(u_ref[...].astype(BF16), h_ref[...])
    act = (0.5 * a_t * (1.0 + lax.erf(a_t * math.sqrt(0.5)))).astype(BF16)
    zero = jnp.zeros((), BF16)
    w_rows = []
    for ii in range(te // PEER_N_KEYS):
        i = e * (te // PEER_N_KEYS) + ii
        gate = None
        for hd in range(PEER_HEADS):
            n1 = n1_ref[hd, pl.ds(i, 1), :].astype(BF16)
            e1 = e1_ref[hd, pl.ds(i, 1), :].astype(BF16)
            term = jnp.where(r2_ref[hd] < n1, e2_ref[hd] * e1, zero)
            gate = term if gate is None else gate + term
        w_rows.append(act[ii * PEER_N_KEYS:(ii + 1) * PEER_N_KEYS] * gate)
    w = jnp.concatenate(w_rows, axis=0) if len(w_rows) > 1 else w_rows[0]
    acc_sc[...] += _dot(vt_ref[...].T, w)

    @pl.when(e == pl.num_programs(1) - 1)
    def _():
        o_ref[...] = acc_sc[...].T.astype(o_ref.dtype)


def _peer_dense(h2, u_bf, vt_bf, r2, n1, e1, e2, tm=1024, te=512):
    T, D = h2.shape
    once = pl.Buffered(1)
    side = (PEER_HEADS, PEER_N_KEYS, tm)
    side_idx = lambda i, e: (0, 0, i)
    return pl.pallas_call(
        _peer_dense_kernel,
        grid=(T // tm, PEER_N_EXPERTS // te),
        in_specs=[
            pl.BlockSpec((tm, D), lambda i, e: (i, 0)),
            pl.BlockSpec((te, D), lambda i, e: (e, 0)),
            pl.BlockSpec((te, D), lambda i, e: (e, 0)),
            pl.BlockSpec(side, side_idx),
            pl.BlockSpec(side, side_idx, pipeline_mode=once),
            pl.BlockSpec(side, side_idx, pipeline_mode=once),
            pl.BlockSpec(side, side_idx),
        ],
        out_specs=pl.BlockSpec((tm, D), lambda i, e: (i, 0), pipeline_mode=once),
        out_shape=jax.ShapeDtypeStruct((T, D), BF16),
        scratch_shapes=[pltpu.VMEM((D, tm), F32)],
        compiler_params=_params("parallel", "arbitrary"),
        name="peer_dense",
    )(h2, u_bf, vt_bf, r2, n1, e1, e2)


def _final_kernel(x_ref, p_ref, g_ref, o_ref):
    o_ref[...] = _rms(x_ref[...] + p_ref[...].astype(F32), g_ref[...])


def _final_norm(x1, peer_out, g, tm=512):
    T, D = x1.shape
    return pl.pallas_call(
        _final_kernel,
        grid=(T // tm,),
        in_specs=[
            pl.BlockSpec((tm, D), lambda i: (i, 0)),
            pl.BlockSpec((tm, D), lambda i: (i, 0)),
            pl.BlockSpec((1, D), lambda i: (0, 0)),
        ],
        out_specs=pl.BlockSpec((tm, D), lambda i: (i, 0)),
        out_shape=jax.ShapeDtypeStruct((T, D), F32),
        compiler_params=_params("parallel"),
        name="final_norm",
    )(x1, peer_out, g)


def _layer(x2, pos_col, pos_row, batch, seq, g_mix, w_in, sinks, g_cq, w_uq, g_ckv, w_ukv, w_a_proj, w_b_proj,
           w_o, g_ffn, w_peer_q, peer_keys, peer_u, peer_v):
    D = x2.shape[1]
    o = np.cumsum([0, SWA_Q_W, SWA_KV_W, SWA_KV_W, MLA_Q_RANK, MLA_KV_RANK, MLA_ROPE_DIM, D, D])
    w_in_t = w_in.T
    w_gates_t = w_in_t[o[6]:o[8]].astype(BF16)
    half = MLA_ROPE_DIM // 2

    q_h, c_q, k_h, v_h, c_kv, k_r2 = _attn_inproj(x2, g_mix.reshape(1, D), w_in_t)
    swa_out = _swa(q_h, k_h, v_h, pos_col, pos_row, sinks.astype(F32), seq)

    w_uq3 = w_uq.reshape(MLA_Q_RANK, MLA_HEADS, MLA_QK_DIM)
    wq_n = w_uq3[:, :, :MLA_NOPE_DIM].reshape(MLA_Q_RANK, MLA_HEADS * MLA_NOPE_DIM).astype(BF16)
    t1 = w_uq3[:, :, MLA_NOPE_DIM:MLA_NOPE_DIM + half]
    t2 = w_uq3[:, :, MLA_NOPE_DIM + half:]
    wq_r = jnp.concatenate([t1, t2], axis=2).reshape(MLA_Q_RANK, MLA_HEADS * MLA_ROPE_DIM).astype(BF16)
    wq_s = jnp.concatenate([-t2, t1], axis=2).reshape(MLA_Q_RANK, MLA_HEADS * MLA_ROPE_DIM).astype(BF16)
    inv = 1.0 / (ROPE_THETA ** (jnp.arange(0, MLA_ROPE_DIM, 2, dtype=F32) / MLA_ROPE_DIM))
    inv128 = jnp.tile(inv, 128 // half).reshape(1, 128)
    q, k, v = _mla_prep(c_q, c_kv, k_r2, pos_col, inv128, g_cq.reshape(1, -1), g_ckv.reshape(1, -1),
                        wq_n, wq_r, wq_s, w_ukv.astype(BF16), batch, seq)
    mla_out = _mla_flash(q, k, v).reshape(batch * seq, MLA_V_W)

    x1 = _mix(x2, g_mix.reshape(1, D), swa_out, mla_out, w_gates_t, w_gates_t,
              w_a_proj.astype(BF16), w_b_proj.astype(BF16), w_o.astype(BF16))

    keys = peer_keys.reshape(PEER_HEADS * 2, PEER_N_KEYS, PEER_HALF).astype(BF16)
    h2, r2, n1, e1, e2 = _peer_route(x1, g_ffn.reshape(1, D), w_peer_q.astype(BF16), keys)
    peer_out = _peer_dense(h2, peer_u, peer_v.astype(BF16), r2, n1, e1, e2)
    return x1, peer_out


def kernel(x, positions, g_mix, w_in, sinks, g_cq, w_uq, g_ckv, w_ukv, w_a_proj, w_b_proj, w_o, g_ffn, w_peer_q,
           peer_keys, peer_u, peer_v, g_final):
    batch, seq, D = x.shape
    depth = g_mix.shape[0]
    assert seq % 1024 == 0 and D % 512 == 0, (seq, D)
    x2 = x.reshape(batch * seq, D)
    pos_f = positions.astype(F32)
    pos_col = pos_f.reshape(batch * seq, 1)
    pos_row = pos_f.reshape(1, batch * seq)
    peer_out = None
    for l in range(depth):
        if peer_out is not None:
            x2 = x2 + peer_out
        x2, peer_out = _layer(x2, pos_col, pos_row, batch, seq, g_mix[l], w_in[l], sinks[l], g_cq[l], w_uq[l],
                              g_ckv[l], w_ukv[l], w_a_proj[l], w_b_proj[l], w_o[l], g_ffn[l], w_peer_q[l],
                              peer_keys[l], peer_u[l], peer_v[l])
    out = _final_norm(x2, peer_out, g_final.reshape(1, D))
    return out.reshape(batch, seq, D)
```

```python
import functools
import math

import numpy as np
import jax
import jax.numpy as jnp
from jax import lax
from jax.experimental import pallas as pl
from jax.experimental.pallas import tpu as pltpu

F32 = jnp.float32
BF16 = jnp.bfloat16

EPS = 1e-6
NEG = -1e30
BLOCK = 128
SWA_HEADS = 16
SWA_KV_HEADS = 4
SWA_GROUP = SWA_HEADS // SWA_KV_HEADS
SWA_HEAD_DIM = 64
SWA_WINDOW = 128
SWA_Q_W = SWA_HEADS * SWA_HEAD_DIM
SWA_KV_W = SWA_KV_HEADS * SWA_HEAD_DIM
MLA_HEADS = 8
MLA_Q_RANK = 512
MLA_KV_RANK = 256
MLA_NOPE_DIM = 128
MLA_ROPE_DIM = 64
MLA_QK_DIM = MLA_NOPE_DIM + MLA_ROPE_DIM
MLA_V_DIM = 128
MLA_V_W = MLA_HEADS * MLA_V_DIM
ROPE_THETA = 10000.0
PEER_HEADS = 8
PEER_N_KEYS = 128
PEER_N_EXPERTS = PEER_N_KEYS * PEER_N_KEYS
PEER_HALF = 128
PEER_TOPK = 16

V7X_VMEM_BYTES = 64 * 1024 * 1024
VMEM_LIMIT = V7X_VMEM_BYTES - 8 * 1024 * 1024


def _params(*semantics):
    return pltpu.CompilerParams(dimension_semantics=semantics, vmem_limit_bytes=VMEM_LIMIT)


def _rms(xf, g):
    y = xf * lax.rsqrt(jnp.mean(xf * xf, axis=-1, keepdims=True) + EPS)
    return y * g


def _dot(a, b):
    return jnp.dot(a, b, preferred_element_type=F32)


def _dot_nt(a, b):
    return lax.dot_general(a, b, (((1,), (1,)), ((), ())), preferred_element_type=F32)


ATTN_COLS = SWA_Q_W + 2 * SWA_KV_W + MLA_Q_RANK + MLA_KV_RANK + MLA_ROPE_DIM
ATTN_WIN = -(-ATTN_COLS // 128) * 128
LOG2E = math.log2(math.e)


def _attn_inproj_kernel(x_ref, g_ref, w_ref, q_out, cq_out, k_out, v_out, ckv_out, kr_out, w_sc):
    @pl.when(pl.program_id(0) == 0)
    def _():
        w_sc[...] = w_ref[...].astype(BF16)

    h = _rms(x_ref[...], g_ref[...]).astype(BF16)
    z = _dot_nt(h, w_sc[...])
    d = SWA_HEAD_DIM
    off = 0
    q = (z[:, :SWA_Q_W] * (d ** -0.5 * LOG2E)).astype(BF16)
    for hd in range(SWA_HEADS):
        q_out[hd] = q[:, hd * d:(hd + 1) * d]
    off += SWA_Q_W
    for hd in range(SWA_KV_HEADS):
        k_out[hd] = z[:, off + hd * d:off + (hd + 1) * d].astype(BF16)
    off += SWA_KV_W
    for hd in range(SWA_KV_HEADS):
        v_out[hd] = z[:, off + hd * d:off + (hd + 1) * d].astype(BF16)
    off += SWA_KV_W
    cq_out[...] = z[:, off:off + MLA_Q_RANK].astype(BF16)
    off += MLA_Q_RANK
    ckv_out[...] = z[:, off:off + MLA_KV_RANK].astype(BF16)
    off += MLA_KV_RANK
    half = MLA_ROPE_DIM // 2
    t1 = z[:, off:off + half]
    t2 = z[:, off + half:off + MLA_ROPE_DIM]
    kr_out[...] = jnp.concatenate([t1, t2, -t2, t1], axis=1).astype(BF16)


def _attn_inproj(x2, g, w_in, tm=512):
    T, D = x2.shape
    d = SWA_HEAD_DIM
    row = lambda i: (i, 0)
    head_major = lambda i: (0, i, 0)
    return pl.pallas_call(
        _attn_inproj_kernel,
        grid=(T // tm,),
        in_specs=[
            pl.BlockSpec((tm, D), row),
            pl.BlockSpec((1, D), lambda i: (0, 0)),
            pl.BlockSpec((ATTN_WIN, D), lambda i: (0, 0), pipeline_mode=pl.Buffered(1)),
        ],
        out_specs=[
            pl.BlockSpec((SWA_HEADS, tm, d), head_major),
            pl.BlockSpec((tm, MLA_Q_RANK), row),
            pl.BlockSpec((SWA_KV_HEADS, tm, d), head_major),
            pl.BlockSpec((SWA_KV_HEADS, tm, d), head_major),
            pl.BlockSpec((tm, MLA_KV_RANK), row),
            pl.BlockSpec((tm, 2 * MLA_ROPE_DIM), row),
        ],
        out_shape=[
            jax.ShapeDtypeStruct((SWA_HEADS, T, d), BF16),
            jax.ShapeDtypeStruct((T, MLA_Q_RANK), BF16),
            jax.ShapeDtypeStruct((SWA_KV_HEADS, T, d), BF16),
            jax.ShapeDtypeStruct((SWA_KV_HEADS, T, d), BF16),
            jax.ShapeDtypeStruct((T, MLA_KV_RANK), BF16),
            jax.ShapeDtypeStruct((T, 2 * MLA_ROPE_DIM), BF16),
        ],
        scratch_shapes=[pltpu.VMEM((ATTN_WIN, D), BF16)],
        compiler_params=_params("arbitrary"),
        name="attn_inproj",
    )(x2, g, w_in)


SWA_TILE = 512
SWA_SUB = SWA_TILE // BLOCK


def _swa_kernel(tiles_per_seq, sinks_ref, q_ref, kc_ref, kp_ref, vc_ref, vp_ref, pc_ref, prc_ref, prp_ref,
                o_ref, kbuf, vbuf, pkbuf):
    i = pl.program_id(0)
    kbuf[:, 0:BLOCK, :] = kp_ref[...]
    kbuf[:, BLOCK:, :] = kc_ref[...]
    vbuf[:, 0:BLOCK, :] = vp_ref[...]
    vbuf[:, BLOCK:, :] = vc_ref[...]
    pkbuf[:, 0:BLOCK] = prp_ref[...]
    pkbuf[:, BLOCK:] = prc_ref[...]
    first_tile = (i % tiles_per_seq) == 0

    qi = lax.broadcasted_iota(jnp.int32, (BLOCK, 2 * BLOCK), 0)
    kj = lax.broadcasted_iota(jnp.int32, (BLOCK, 2 * BLOCK), 1)
    rel = BLOCK + qi - kj
    band = (rel >= 0) & (rel < SWA_WINDOW)
    d = SWA_HEAD_DIM
    ones_rhs = jnp.ones((2 * BLOCK, 128), BF16)

    for c in range(SWA_SUB):
        r0 = c * BLOCK
        pq = pc_ref[r0:r0 + BLOCK, :]
        pk = pkbuf[:, r0:r0 + 2 * BLOCK]
        dist = jnp.abs(pq - pk) * LOG2E
        if c == 0:
            mask = band & (jnp.logical_not(first_tile) | (kj >= BLOCK))
        else:
            mask = band
        for g in range(SWA_KV_HEADS):
            qs = jnp.concatenate([q_ref[g * SWA_GROUP + hh, r0:r0 + BLOCK, :] for hh in range(SWA_GROUP)], axis=0)
            logits = _dot_nt(qs, kbuf[g, r0:r0 + 2 * BLOCK, :])
            probs, denoms = [], []
            for hh in range(SWA_GROUP):
                h = g * SWA_GROUP + hh
                slope = 2.0 ** (-8.0 * (h + 1) / SWA_HEADS)
                lh = logits[hh * BLOCK:(hh + 1) * BLOCK] - slope * dist
                lh = jnp.where(mask, lh, NEG)
                sink = sinks_ref[h] * LOG2E
                m = jnp.maximum(jnp.max(lh, axis=-1, keepdims=True), sink)
                p = jnp.exp2(lh - m)
                denoms.append(jnp.exp2(sink - m))
                probs.append(p.astype(BF16))
            pm = jnp.concatenate(probs, axis=0)
            o = _dot(pm, vbuf[g, r0:r0 + 2 * BLOCK, :])
            row_sum = _dot(pm, ones_rhs)
            for hh in range(SWA_GROUP):
                h = g * SWA_GROUP + hh
                denom = row_sum[hh * BLOCK:(hh + 1) * BLOCK, :d] + denoms[hh]
                o_ref[r0:r0 + BLOCK, h * d:(h + 1) * d] = (o[hh * BLOCK:(hh + 1) * BLOCK] / denom).astype(o_ref.dtype)


def _swa(q_h, k_h, v_h, pos_col, pos_row, sinks, seq):
    T = q_h.shape[1]
    d = SWA_HEAD_DIM
    tiles_per_seq = seq // SWA_TILE

    def prev_blk(i):
        return jnp.where(i % tiles_per_seq == 0, i * SWA_SUB, i * SWA_SUB - 1)

    cur = lambda i: (0, i, 0)
    prev = lambda i: (0, prev_blk(i), 0)
    return pl.pallas_call(
        functools.partial(_swa_kernel, tiles_per_seq),
        grid=(T // SWA_TILE,),
        in_specs=[
            pl.BlockSpec(memory_space=pltpu.SMEM),
            pl.BlockSpec((SWA_HEADS, SWA_TILE, d), cur),
            pl.BlockSpec((SWA_KV_HEADS, SWA_TILE, d), cur),
            pl.BlockSpec((SWA_KV_HEADS, BLOCK, d), prev),
            pl.BlockSpec((SWA_KV_HEADS, SWA_TILE, d), cur),
            pl.BlockSpec((SWA_KV_HEADS, BLOCK, d), prev),
            pl.BlockSpec((SWA_TILE, 1), lambda i: (i, 0)),
            pl.BlockSpec((1, SWA_TILE), lambda i: (0, i)),
            pl.BlockSpec((1, BLOCK), lambda i: (0, prev_blk(i))),
        ],
        out_specs=pl.BlockSpec((SWA_TILE, SWA_Q_W), lambda i: (i, 0)),
        out_shape=jax.ShapeDtypeStruct((T, SWA_Q_W), BF16),
        scratch_shapes=[
            pltpu.VMEM((SWA_KV_HEADS, SWA_TILE + BLOCK, d), BF16),
            pltpu.VMEM((SWA_KV_HEADS, SWA_TILE + BLOCK, d), BF16),
            pltpu.VMEM((1, SWA_TILE + BLOCK), F32),
        ],
        compiler_params=_params("parallel"),
        name="swa",
    )(sinks, q_h, k_h, k_h, v_h, v_h, pos_col, pos_row, pos_row)


MLA_VP = 2 * MLA_V_DIM


def _mla_prep_kernel(cq_ref, ckv_ref, kr_ref, pos_ref, inv_ref, gcq_ref, gckv_ref, wqn_ref, wqr_ref, wqs_ref,
                     wkv_ref, q_out, k_out, v_out):
    scale = MLA_QK_DIM ** -0.5 * math.log2(math.e)
    ang = pos_ref[...] * inv_ref[...]
    cos = jnp.cos(ang)
    sin = jnp.sin(ang)
    cos4 = jnp.concatenate([cos] * 4, axis=1)
    sin4 = jnp.concatenate([sin] * 4, axis=1)

    cn = _rms(cq_ref[...].astype(F32), gcq_ref[...]).astype(BF16)
    qn = _dot(cn, wqn_ref[...])
    qr = _dot(cn, wqr_ref[...])
    qs = _dot(cn, wqs_ref[...])
    qrot = qr * cos4 + qs * sin4

    kvn = _rms(ckv_ref[...].astype(F32), gckv_ref[...]).astype(BF16)
    kv = _dot(kvn, wkv_ref[...])
    kr = kr_ref[...].astype(F32)
    krot = (kr[:, :MLA_ROPE_DIM] * cos[:, :MLA_ROPE_DIM]
            + kr[:, MLA_ROPE_DIM:] * sin[:, :MLA_ROPE_DIM]).astype(BF16)

    lane = lax.broadcasted_iota(jnp.int32, (kr.shape[0], MLA_VP - MLA_V_DIM), 1)
    ones_col = jnp.where(lane == 0, 1.0, 0.0).astype(BF16)
    for h in range(MLA_HEADS):
        q_h = jnp.concatenate(
            [qn[:, h * MLA_NOPE_DIM:(h + 1) * MLA_NOPE_DIM], qrot[:, h * MLA_ROPE_DIM:(h + 1) * MLA_ROPE_DIM]],
            axis=1) * scale
        q_out[0, h] = q_h.astype(BF16)
        kn_h = kv[:, h * 256:h * 256 + MLA_NOPE_DIM].astype(BF16)
        k_out[0, h] = jnp.concatenate([kn_h, krot], axis=1)
        v_out[0, h, :, :MLA_V_DIM] = kv[:, h * 256 + MLA_NOPE_DIM:(h + 1) * 256].astype(BF16)
        v_out[0, h, :, MLA_V_DIM:] = ones_col


def _mla_prep(c_q, c_kv, k_r2, pos_col, inv128, g_cq, g_ckv, wq_n, wq_r, wq_s, w_kv, batch, seq, tm=512):
    T = c_q.shape[0]
    tps = seq // tm
    const = lambda i: (0, 0)
    out_idx = lambda i: (i // tps, 0, i % tps, 0)
    return pl.pallas_call(
        _mla_prep_kernel,
        grid=(T // tm,),
        in_specs=[
            pl.BlockSpec((tm, MLA_Q_RANK), lambda i: (i, 0)),
            pl.BlockSpec((tm, MLA_KV_RANK), lambda i: (i, 0)),
            pl.BlockSpec((tm, 2 * MLA_ROPE_DIM), lambda i: (i, 0)),
            pl.BlockSpec((tm, 1), lambda i: (i, 0)),
            pl.BlockSpec((1, 128), const),
            pl.BlockSpec((1, MLA_Q_RANK), const),
            pl.BlockSpec((1, MLA_KV_RANK), const),
            pl.BlockSpec(wq_n.shape, const),
            pl.BlockSpec(wq_r.shape, const),
            pl.BlockSpec(wq_s.shape, const),
            pl.BlockSpec(w_kv.shape, const),
        ],
        out_specs=[
            pl.BlockSpec((1, MLA_HEADS, tm, MLA_QK_DIM), out_idx),
            pl.BlockSpec((1, MLA_HEADS, tm, MLA_QK_DIM), out_idx),
            pl.BlockSpec((1, MLA_HEADS, tm, MLA_VP), out_idx),
        ],
        out_shape=[
            jax.ShapeDtypeStruct((batch, MLA_HEADS, seq, MLA_QK_DIM), BF16),
            jax.ShapeDtypeStruct((batch, MLA_HEADS, seq, MLA_QK_DIM), BF16),
            jax.ShapeDtypeStruct((batch, MLA_HEADS, seq, MLA_VP), BF16),
        ],
        compiler_params=_params("parallel"),
        name="mla_prep",
    )(c_q, c_kv, k_r2, pos_col, inv128, g_cq, g_ckv, wq_n, wq_r, wq_s, w_kv)


MLA_TILE = 512
MLA_HP = 2


def _mla_flash_kernel(q_ref, k_ref, v_ref, o_ref, s_sc, p_sc, alpha_sc, m_sc, acc_sc):
    qi = pl.program_id(2)
    t = q_ref.shape[2]
    heads = range(MLA_HP)

    def scores(hh, j):
        r0 = pl.multiple_of(j * t, t)
        return _dot_nt(q_ref[0, hh], k_ref[0, hh, pl.ds(r0, t), :])

    def softmax(hh, s, slot):
        m_prev = m_sc[hh]
        m_new = jnp.maximum(m_prev, jnp.max(s, axis=1, keepdims=True))
        alpha_sc[slot, hh] = jnp.exp2(m_prev - m_new)
        p_sc[slot, hh] = jnp.exp2(s - jnp.concatenate([m_new] * (t // 128), axis=1)).astype(BF16)
        m_sc[hh] = m_new

    def accumulate(hh, j, slot):
        r0 = pl.multiple_of(j * t, t)
        pv = _dot(p_sc[slot, hh], v_ref[0, hh, pl.ds(r0, t), :])
        acc_sc[hh] = jnp.concatenate([alpha_sc[slot, hh]] * (MLA_VP // 128), axis=1) * acc_sc[hh] + pv

    m_sc[...] = jnp.full(m_sc.shape, NEG, F32)
    acc_sc[...] = jnp.zeros(acc_sc.shape, F32)
    p_sc[1] = jnp.zeros(p_sc.shape[1:], BF16)
    alpha_sc[1] = jnp.ones(alpha_sc.shape[1:], F32)
    for hh in heads:
        s_sc[0, hh] = scores(hh, 0)

    def stage(j, slot):
        for hh in heads:
            s_sc[1 - slot, hh] = scores(hh, j + 1)
            softmax(hh, s_sc[slot, hh], slot)
            accumulate(hh, jnp.maximum(j - 1, 0), 1 - slot)

    def body(i, carry):
        for k in range(4):
            stage(4 * i + k, k % 2)
        return carry

    lax.fori_loop(0, qi // 4, body, 0)

    @pl.when(qi % 4 >= 2)
    def _():
        base = (qi // 4) * 4
        stage(base, 0)
        stage(base + 1, 1)

    def finish(slot):
        row = lax.broadcasted_iota(jnp.int32, (t, t), 0)
        col = lax.broadcasted_iota(jnp.int32, (t, t), 1)
        for hh in heads:
            softmax(hh, jnp.where(col <= row, s_sc[slot, hh], NEG), slot)
            accumulate(hh, jnp.maximum(qi - 1, 0), 1 - slot)
        for hh in heads:
            accumulate(hh, qi, slot)
            acc = acc_sc[hh]
            o_ref[0, :, hh * MLA_V_DIM:(hh + 1) * MLA_V_DIM] = (
                acc[:, :MLA_V_DIM] / acc[:, MLA_V_DIM:MLA_V_DIM + 1]).astype(o_ref.dtype)

    @pl.when(qi % 2 == 0)
    def _():
        finish(0)

    @pl.when(qi % 2 == 1)
    def _():
        stage(qi - 1, 0)
        finish(1)


def _mla_flash(q, k, v):
    B, H, S, _ = q.shape
    t = MLA_TILE
    hp = MLA_HP
    return pl.pallas_call(
        _mla_flash_kernel,
        grid=(B, H // hp, S // t),
        in_specs=[
            pl.BlockSpec((1, hp, t, MLA_QK_DIM), lambda b, h, i: (b, h, i, 0)),
            pl.BlockSpec((1, hp, S, MLA_QK_DIM), lambda b, h, i: (b, h, 0, 0)),
            pl.BlockSpec((1, hp, S, MLA_VP), lambda b, h, i: (b, h, 0, 0)),
        ],
        out_specs=pl.BlockSpec((1, t, hp * MLA_V_DIM), lambda b, h, i: (b, i, h)),
        out_shape=jax.ShapeDtypeStruct((B, S, MLA_V_W), BF16),
        scratch_shapes=[
            pltpu.VMEM((2, hp, t, t), F32),
            pltpu.VMEM((2, hp, t, t), BF16),
            pltpu.VMEM((2, hp, t, 128), F32),
            pltpu.VMEM((hp, t, 128), F32),
            pltpu.VMEM((hp, t, MLA_VP), F32),
        ],
        compiler_params=_params("parallel", "parallel", "arbitrary"),
        name="mla_flash",
    )(q, k, v)


def _mix_kernel(x_ref, g_ref, ya_in_ref, yb_in_ref, wga_ref, wgb_ref, wa_ref, wb_ref, wo_ref, o_ref, h_sc, acc_sc):
    n = pl.program_id(1)

    @pl.when(n == 0)
    def _():
        h_sc[...] = _rms(x_ref[...], g_ref[...]).astype(BF16)
        acc_sc[...] = jnp.zeros(acc_sc.shape, F32)

    h = h_sc[...]
    ga = _dot_nt(h, wga_ref[...])
    gb = _dot_nt(h, wgb_ref[...])
    ya = _dot(ya_in_ref[...], wa_ref[...])
    yb = _dot(yb_in_ref[...], wb_ref[...])
    mixed = jax.nn.sigmoid(ga) * ya + jax.nn.sigmoid(gb) * yb
    acc_sc[...] += _dot(mixed.astype(BF16), wo_ref[...])

    @pl.when(n == pl.num_programs(1) - 1)
    def _():
        o_ref[...] = x_ref[...] + acc_sc[...]


def _mix(x2, g, swa_out, mla_out, w_ga, w_gb, w_a, w_b, w_o, tm=512, tn=512):
    T, D = x2.shape
    return pl.pallas_call(
        _mix_kernel,
        grid=(T // tm, D // tn),
        in_specs=[
            pl.BlockSpec((tm, D), lambda i, n: (i, 0)),
            pl.BlockSpec((1, D), lambda i, n: (0, 0)),
            pl.BlockSpec((tm, SWA_Q_W), lambda i, n: (i, 0)),
            pl.BlockSpec((tm, MLA_V_W), lambda i, n: (i, 0)),
            pl.BlockSpec((tn, D), lambda i, n: (n, 0)),
            pl.BlockSpec((tn, D), lambda i, n: (n + D // tn, 0)),
            pl.BlockSpec((SWA_Q_W, tn), lambda i, n: (0, n)),
            pl.BlockSpec((MLA_V_W, tn), lambda i, n: (0, n)),
            pl.BlockSpec((tn, D), lambda i, n: (n, 0)),
        ],
        out_specs=pl.BlockSpec((tm, D), lambda i, n: (i, 0)),
        out_shape=jax.ShapeDtypeStruct((T, D), F32),
        scratch_shapes=[pltpu.VMEM((tm, D), BF16), pltpu.VMEM((tm, D), F32)],
        compiler_params=_params("parallel", "arbitrary"),
        name="mix",
    )(x2, g, swa_out, mla_out, w_ga, w_gb, w_a, w_b, w_o)


PEER_LANES = 128
CAND_ROWS = 16 + 7 * 8 + 8


def _extract_top(s, iters, exact):
    n_rows = s.shape[0]
    rows = lax.broadcasted_iota(jnp.int32, s.shape, 0)
    rank = jnp.full(s.shape, float(iters), F32)
    vals = []
    for r in range(iters):
        m = jnp.max(s, axis=0, keepdims=True)
        if exact:
            idx = jnp.min(jnp.where(s == m, rows, n_rows), axis=0, keepdims=True)
            sel = rows == idx
        else:
            sel = s == m
        rank = jnp.where(sel, float(r), rank)
        s = jnp.where(sel, -jnp.inf, s)
        vals.append(m)
    return rank, vals


def _ranked_excess(rank, iters):
    return jnp.sum(jnp.where(rank < float(iters), 1.0, 0.0), axis=0, keepdims=True) - float(iters)


def _peer_route_kernel(x_ref, g_ref, wq_ref, keys_ref, h_out, r2_out, n1_out, e1_out, e2_out, q_sc):
    tm = x_ref.shape[0]
    h = _rms(x_ref[...], g_ref[...]).astype(BF16)
    h_out[...] = h
    q_sc[...] = _dot(h, wq_ref[...]).astype(BF16)

    def route(hd, t0, exact):
        c0 = pl.multiple_of(hd * 2 * PEER_HALF, 2 * PEER_HALF)
        q1 = q_sc[t0:t0 + PEER_LANES, pl.ds(c0, PEER_HALF)]
        q2 = q_sc[t0:t0 + PEER_LANES, pl.ds(c0 + PEER_HALF, PEER_HALF)]
        s1 = _dot_nt(keys_ref[2 * hd], q1)
        s2 = _dot_nt(keys_ref[2 * hd + 1], q2)
        r1, v1 = _extract_top(s1, PEER_TOPK, exact)
        r2, v2 = _extract_top(s2, PEER_TOPK, exact)
        v2_lo = jnp.concatenate(v2[:8], axis=0)
        v2_all = jnp.concatenate(v2, axis=0)
        slabs = [v1[0] + v2_all]
        for a in range(1, 8):
            slabs.append(v1[a] + v2_lo)
        slabs.append(jnp.concatenate(v1[8:], axis=0) + v2[0])
        cand = jnp.concatenate(slabs, axis=0)
        rc, vc = _extract_top(cand, PEER_TOPK, exact)
        picked = jnp.where(rc < float(PEER_TOPK), 1.0, 0.0)
        z = jnp.ones_like(vc[0])
        for r in range(1, PEER_TOPK):
            z = z + jnp.exp(vc[r] - vc[0])
        cnt_lo = picked[0:8]
        for a in range(1, 8):
            cnt_lo = cnt_lo + picked[8 + 8 * a:16 + 8 * a]
        first_col = jnp.sum(picked[72:80], axis=0, keepdims=True)
        cnt_hi = jnp.sum(picked[8:16], axis=0, keepdims=True)
        n1 = jnp.where(r1 == 0.0, cnt_hi, 0.0)
        for b in range(8):
            cnt_b = cnt_lo[b:b + 1] + first_col if b == 0 else cnt_lo[b:b + 1]
            n1 = n1 + jnp.where(r1 < cnt_b, 1.0, 0.0)
        r2_out[hd, :, t0:t0 + PEER_LANES] = r2.astype(r2_out.dtype)
        n1_out[hd, :, t0:t0 + PEER_LANES] = n1
        e1_out[hd, :, t0:t0 + PEER_LANES] = jnp.exp(s1 - v1[0])
        e2_out[hd, :, t0:t0 + PEER_LANES] = (jnp.exp(s2 - v2[0]) / z).astype(e2_out.dtype)
        if exact:
            return None
        excess = (_ranked_excess(r1, PEER_TOPK) + _ranked_excess(r2, PEER_TOPK)
                  + _ranked_excess(rc, PEER_TOPK))
        return jnp.max(excess)

    def head_body(hd, carry):
        starts = [ch * PEER_LANES for ch in range(tm // PEER_LANES)]
        ties = [route(hd, t0, exact=False) for t0 in starts]

        @pl.when(functools.reduce(jnp.maximum, ties) > 0.0)
        def _():
            for t0 in starts:
                route(hd, t0, exact=True)
        return carry

    lax.fori_loop(0, PEER_HEADS, head_body, 0)


def _peer_route(x1, g, w_q, keys, tm=512):
    T, D = x1.shape
    side = jax.ShapeDtypeStruct((PEER_HEADS, PEER_N_KEYS, T), F32)
    side_bf = jax.ShapeDtypeStruct((PEER_HEADS, PEER_N_KEYS, T), BF16)
    side_spec = pl.BlockSpec((PEER_HEADS, PEER_N_KEYS, tm), lambda i: (0, 0, i))
    return pl.pallas_call(
        _peer_route_kernel,
        grid=(T // tm,),
        in_specs=[
            pl.BlockSpec((tm, D), lambda i: (i, 0)),
            pl.BlockSpec((1, D), lambda i: (0, 0)),
            pl.BlockSpec(w_q.shape, lambda i: (0, 0)),
            pl.BlockSpec(keys.shape, lambda i: (0, 0, 0)),
        ],
        out_specs=[pl.BlockSpec((tm, D), lambda i: (i, 0)), side_spec, side_spec, side_spec, side_spec],
        out_shape=[jax.ShapeDtypeStruct((T, D), BF16), side_bf, side, side, side_bf],
        scratch_shapes=[pltpu.VMEM((tm, w_q.shape[1]), BF16)],
        compiler_params=_params("parallel"),
        name="peer_route",
    )(x1, g, w_q, keys)


def _peer_dense_kernel(h_ref, u_ref, vt_ref, r2_ref, n1_ref, e1_ref, e2_ref, o_ref, acc_sc):
    e = pl.program_id(1)
    te = u_ref.shape[0]

    @pl.when(e == 0)
    def _():
        acc_sc[...] = jnp.zeros(acc_sc.shape, F32)

    a_t = _dot_nt(u_ref[...].astype(BF16), h_ref[...])
    act = (0.5 * a_t * (1.0 + lax.erf(a_t * math.sqrt(0.5)))).astype(BF16)
    zero = jnp.zeros((), BF16)
    w_rows = []
    for ii in range(te // PEER_N_KEYS):
        i = e * (te // PEER_N_KEYS) + ii
        gate = None
        for hd in range(PEER_HEADS):
            n1 = n1_ref[hd, pl.ds(i, 1), :].astype(BF16)
            e1 = e1_ref[hd, pl.ds(i, 1), :].astype(BF16)
            term = jnp.where(r2_ref[hd] < n1, e2_ref[hd] * e1, zero)
            gate = term if gate is None else gate + term
        w_rows.append(act[ii * PEER_N_KEYS:(ii + 1) * PEER_N_KEYS] * gate)
    w = jnp.concatenate(w_rows, axis=0) if len(w_rows) > 1 else w_rows[0]
    acc_sc[...] += _dot(vt_ref[...].T, w)

    @pl.when(e == pl.num_programs(1) - 1)
    def _():
        o_ref[...] = acc_sc[...].T.astype(o_ref.dtype)


def _peer_dense(h2, u_bf, vt_bf, r2, n1, e1, e2, tm=1024, te=512):
    T, D = h2.shape
    once = pl.Buffered(1)
    side = (PEER_HEADS, PEER_N_KEYS, tm)
    side_idx = lambda i, e: (0, 0, i)
    return pl.pallas_call(
        _peer_dense_kernel,
        grid=(T // tm, PEER_N_EXPERTS // te),
        in_specs=[
            pl.BlockSpec((tm, D), lambda i, e: (i, 0)),
            pl.BlockSpec((te, D), lambda i, e: (e, 0)),
            pl.BlockSpec((te, D), lambda i, e: (e, 0)),
            pl.BlockSpec(side, side_idx),
            pl.BlockSpec(side, side_idx, pipeline_mode=once),
            pl.BlockSpec(side, side_idx, pipeline_mode=once),
            pl.BlockSpec(side, side_idx),
        ],
        out_specs=pl.BlockSpec((tm, D), lambda i, e: (i, 0), pipeline_mode=once),
        out_shape=jax.ShapeDtypeStruct((T, D), BF16),
        scratch_shapes=[pltpu.VMEM((D, tm), F32)],
        compiler_params=_params("parallel", "arbitrary"),
        name="peer_dense",
    )(h2, u_bf, vt_bf, r2, n1, e1, e2)


def _final_kernel(x_ref, p_ref, g_ref, o_ref):
    o_ref[...] = _rms(x_ref[...] + p_ref[...].astype(F32), g_ref[...])


def _final_norm(x1, peer_out, g, tm=512):
    T, D = x1.shape
    return pl.pallas_call(
        _final_kernel,
        grid=(T // tm,),
        in_specs=[
            pl.BlockSpec((tm, D), lambda i: (i, 0)),
            pl.BlockSpec((tm, D), lambda i: (i, 0)),
            pl.BlockSpec((1, D), lambda i: (0, 0)),
        ],
        out_specs=pl.BlockSpec((tm, D), lambda i: (i, 0)),
        out_shape=jax.ShapeDtypeStruct((T, D), F32),
        compiler_params=_params("parallel"),
        name="final_norm",
    )(x1, peer_out, g)


def _layer(x2, pos_col, pos_row, batch, seq, g_mix, w_in, sinks, g_cq, w_uq, g_ckv, w_ukv, w_a_proj, w_b_proj,
           w_o, g_ffn, w_peer_q, peer_keys, peer_u, peer_v):
    D = x2.shape[1]
    o = np.cumsum([0, SWA_Q_W, SWA_KV_W, SWA_KV_W, MLA_Q_RANK, MLA_KV_RANK, MLA_ROPE_DIM, D, D])
    w_in_t = w_in.T
    w_gates_t = w_in_t[o[6]:o[8]].astype(BF16)
    half = MLA_ROPE_DIM // 2

    q_h, c_q, k_h, v_h, c_kv, k_r2 = _attn_inproj(x2, g_mix.reshape(1, D), w_in_t)
    swa_out = _swa(q_h, k_h, v_h, pos_col, pos_row, sinks.astype(F32), seq)

    w_uq3 = w_uq.reshape(MLA_Q_RANK, MLA_HEADS, MLA_QK_DIM)
    wq_n = w_uq3[:, :, :MLA_NOPE_DIM].reshape(MLA_Q_RANK, MLA_HEADS * MLA_NOPE_DIM).astype(BF16)
    t1 = w_uq3[:, :, MLA_NOPE_DIM:MLA_NOPE_DIM + half]
    t2 = w_uq3[:, :, MLA_NOPE_DIM + half:]
    wq_r = jnp.concatenate([t1, t2], axis=2).reshape(MLA_Q_RANK, MLA_HEADS * MLA_ROPE_DIM).astype(BF16)
    wq_s = jnp.concatenate([-t2, t1], axis=2).reshape(MLA_Q_RANK, MLA_HEADS * MLA_ROPE_DIM).astype(BF16)
    inv = 1.0 / (ROPE_THETA ** (jnp.arange(0, MLA_ROPE_DIM, 2, dtype=F32) / MLA_ROPE_DIM))
    inv128 = jnp.tile(inv, 128 // half).reshape(1, 128)
    q, k, v = _mla_prep(c_q, c_kv, k_r2, pos_col, inv128, g_cq.reshape(1, -1), g_ckv.reshape(1, -1),
                        wq_n, wq_r, wq_s, w_ukv.astype(BF16), batch, seq)
    mla_out = _mla_flash(q, k, v).reshape(batch * seq, MLA_V_W)

    x1 = _mix(x2, g_mix.reshape(1, D), swa_out, mla_out, w_gates_t, w_gates_t,
              w_a_proj.astype(BF16), w_b_proj.astype(BF16), w_o.astype(BF16))

    keys = peer_keys.reshape(PEER_HEADS * 2, PEER_N_KEYS, PEER_HALF).astype(BF16)
    h2, r2, n1, e1, e2 = _peer_route(x1, g_ffn.reshape(1, D), w_peer_q.astype(BF16), keys)
    peer_out = _peer_dense(h2, peer_u, peer_v.astype(BF16), r2, n1, e1, e2)
    return x1, peer_out


def kernel(x, positions, g_mix, w_in, sinks, g_cq, w_uq, g_ckv, w_ukv, w_a_proj, w_b_proj, w_o, g_ffn, w_peer_q,
           peer_keys, peer_u, peer_v, g_final):
    batch, seq, D = x.shape
    depth = g_mix.shape[0]
    assert seq % 1024 == 0 and D % 512 == 0, (seq, D)
    x2 = x.reshape(batch * seq, D)
    pos_f = positions.astype(F32)
    pos_col = pos_f.reshape(batch * seq, 1)
    pos_row = pos_f.reshape(1, batch * seq)
    peer_out = None
    for l in range(depth):
        if peer_out is not None:
            x2 = x2 + peer_out
        x2, peer_out = _layer(x2, pos_col, pos_row, batch, seq, g_mix[l], w_in[l], sinks[l], g_cq[l], w_uq[l],
                              g_ckv[l], w_ukv[l], w_a_proj[l], w_b_proj[l], w_o[l], g_ffn[l], w_peer_q[l],
                              peer_keys[l], peer_u[l], peer_v[l])
    out = _final_norm(x2, peer_out, g_final.reshape(1, D))
    return out.reshape(batch, seq, D)
```

```python
import functools
import math

import numpy as np
import jax
import jax.numpy as jnp
from jax import lax
from jax.experimental import pallas as pl
from jax.experimental.pallas import tpu as pltpu

F32 = jnp.float32
BF16 = jnp.bfloat16

EPS = 1e-6
NEG = -1e30
BLOCK = 128
SWA_HEADS = 16
SWA_KV_HEADS = 4
SWA_GROUP = SWA_HEADS // SWA_KV_HEADS
SWA_HEAD_DIM = 64
SWA_WINDOW = 128
SWA_Q_W = SWA_HEADS * SWA_HEAD_DIM
SWA_KV_W = SWA_KV_HEADS * SWA_HEAD_DIM
MLA_HEADS = 8
MLA_Q_RANK = 512
MLA_KV_RANK = 256
MLA_NOPE_DIM = 128
MLA_ROPE_DIM = 64
MLA_QK_DIM = MLA_NOPE_DIM + MLA_ROPE_DIM
MLA_V_DIM = 128
MLA_V_W = MLA_HEADS * MLA_V_DIM
ROPE_THETA = 10000.0
PEER_HEADS = 8
PEER_N_KEYS = 128
PEER_N_EXPERTS = PEER_N_KEYS * PEER_N_KEYS
PEER_HALF = 128
PEER_TOPK = 16

V7X_VMEM_BYTES = 64 * 1024 * 1024
VMEM_LIMIT = V7X_VMEM_BYTES - 8 * 1024 * 1024


def _params(*semantics):
    return pltpu.CompilerParams(dimension_semantics=semantics, vmem_limit_bytes=VMEM_LIMIT)


def _rms(xf, g):
    y = xf * lax.rsqrt(jnp.mean(xf * xf, axis=-1, keepdims=True) + EPS)
    return y * g


def _dot(a, b):
    return jnp.dot(a, b, preferred_element_type=F32)


def _dot_nt(a, b):
    return lax.dot_general(a, b, (((1,), (1,)), ((), ())), preferred_element_type=F32)


ATTN_COLS = SWA_Q_W + 2 * SWA_KV_W + MLA_Q_RANK + MLA_KV_RANK + MLA_ROPE_DIM
ATTN_WIN = -(-ATTN_COLS // 128) * 128
LOG2E = math.log2(math.e)


def _attn_inproj_kernel(x_ref, g_ref, w_ref, q_out, cq_out, k_out, v_out, ckv_out, kr_out, w_sc):
    @pl.when(pl.program_id(0) == 0)
    def _():
        w_sc[...] = w_ref[...].astype(BF16)

    h = _rms(x_ref[...], g_ref[...]).astype(BF16)
    z = _dot_nt(h, w_sc[...])
    d = SWA_HEAD_DIM
    off = 0
    q = (z[:, :SWA_Q_W] * (d ** -0.5 * LOG2E)).astype(BF16)
    for hd in range(SWA_HEADS):
        q_out[hd] = q[:, hd * d:(hd + 1) * d]
    off += SWA_Q_W
    for hd in range(SWA_KV_HEADS):
        k_out[hd] = z[:, off + hd * d:off + (hd + 1) * d].astype(BF16)
    off += SWA_KV_W
    for hd in range(SWA_KV_HEADS):
        v_out[hd] = z[:, off + hd * d:off + (hd + 1) * d].astype(BF16)
    off += SWA_KV_W
    cq_out[...] = z[:, off:off + MLA_Q_RANK].astype(BF16)
    off += MLA_Q_RANK
    ckv_out[...] = z[:, off:off + MLA_KV_RANK].astype(BF16)
    off += MLA_KV_RANK
    half = MLA_ROPE_DIM // 2
    t1 = z[:, off:off + half]
    t2 = z[:, off + half:off + MLA_ROPE_DIM]
    kr_out[...] = jnp.concatenate([t1, t2, -t2, t1], axis=1).astype(BF16)


def _attn_inproj(x2, g, w_in, tm=512):
    T, D = x2.shape
    d = SWA_HEAD_DIM
    row = lambda i: (i, 0)
    head_major = lambda i: (0, i, 0)
    return pl.pallas_call(
        _attn_inproj_kernel,
        grid=(T // tm,),
        in_specs=[
            pl.BlockSpec((tm, D), row),
            pl.BlockSpec((1, D), lambda i: (0, 0)),
            pl.BlockSpec((ATTN_WIN, D), lambda i: (0, 0), pipeline_mode=pl.Buffered(1)),
        ],
        out_specs=[
            pl.BlockSpec((SWA_HEADS, tm, d), head_major),
            pl.BlockSpec((tm, MLA_Q_RANK), row),
            pl.BlockSpec((SWA_KV_HEADS, tm, d), head_major),
            pl.BlockSpec((SWA_KV_HEADS, tm, d), head_major),
            pl.BlockSpec((tm, MLA_KV_RANK), row),
            pl.BlockSpec((tm, 2 * MLA_ROPE_DIM), row),
        ],
        out_shape=[
            jax.ShapeDtypeStruct((SWA_HEADS, T, d), BF16),
            jax.ShapeDtypeStruct((T, MLA_Q_RANK), BF16),
            jax.ShapeDtypeStruct((SWA_KV_HEADS, T, d), BF16),
            jax.ShapeDtypeStruct((SWA_KV_HEADS, T, d), BF16),
            jax.ShapeDtypeStruct((T, MLA_KV_RANK), BF16),
            jax.ShapeDtypeStruct((T, 2 * MLA_ROPE_DIM), BF16),
        ],
        scratch_shapes=[pltpu.VMEM((ATTN_WIN, D), BF16)],
        compiler_params=_params("arbitrary"),
        name="attn_inproj",
    )(x2, g, w_in)


SWA_TILE = 512
SWA_SUB = SWA_TILE // BLOCK


def _swa_kernel(tiles_per_seq, sinks_ref, q_ref, kc_ref, kp_ref, vc_ref, vp_ref, pc_ref, prc_ref, prp_ref,
                o_ref, kbuf, vbuf, pkbuf):
    i = pl.program_id(0)
    kbuf[:, 0:BLOCK, :] = kp_ref[...]
    kbuf[:, BLOCK:, :] = kc_ref[...]
    vbuf[:, 0:BLOCK, :] = vp_ref[...]
    vbuf[:, BLOCK:, :] = vc_ref[...]
    pkbuf[:, 0:BLOCK] = prp_ref[...]
    pkbuf[:, BLOCK:] = prc_ref[...]
    first_tile = (i % tiles_per_seq) == 0

    qi = lax.broadcasted_iota(jnp.int32, (BLOCK, 2 * BLOCK), 0)
    kj = lax.broadcasted_iota(jnp.int32, (BLOCK, 2 * BLOCK), 1)
    rel = BLOCK + qi - kj
    band = (rel >= 0) & (rel < SWA_WINDOW)
    d = SWA_HEAD_DIM
    ones_rhs = jnp.ones((2 * BLOCK, 128), BF16)

    for c in range(SWA_SUB):
        r0 = c * BLOCK
        pq = pc_ref[r0:r0 + BLOCK, :]
        pk = pkbuf[:, r0:r0 + 2 * BLOCK]
        dist = jnp.abs(pq - pk) * LOG2E
        if c == 0:
            mask = band & (jnp.logical_not(first_tile) | (kj >= BLOCK))
        else:
            mask = band
        for g in range(SWA_KV_HEADS):
            qs = jnp.concatenate([q_ref[g * SWA_GROUP + hh, r0:r0 + BLOCK, :] for hh in range(SWA_GROUP)], axis=0)
            logits = _dot_nt(qs, kbuf[g, r0:r0 + 2 * BLOCK, :])
            probs, denoms = [], []
            for hh in range(SWA_GROUP):
                h = g * SWA_GROUP + hh
                slope = 2.0 ** (-8.0 * (h + 1) / SWA_HEADS)
                lh = logits[hh * BLOCK:(hh + 1) * BLOCK] - slope * dist
                lh = jnp.where(mask, lh, NEG)
                sink = sinks_ref[h] * LOG2E
                m = jnp.maximum(jnp.max(lh, axis=-1, keepdims=True), sink)
                p = jnp.exp2(lh - m)
                denoms.append(jnp.exp2(sink - m))
                probs.append(p.astype(BF16))
            pm = jnp.concatenate(probs, axis=0)
            o = _dot(pm, vbuf[g, r0:r0 + 2 * BLOCK, :])
            row_sum = _dot(pm, ones_rhs)
            for hh in range(SWA_GROUP):
                h = g * SWA_GROUP + hh
                denom = row_sum[hh * BLOCK:(hh + 1) * BLOCK, :d] + denoms[hh]
                o_ref[r0:r0 + BLOCK, h * d:(h + 1) * d] = (o[hh * BLOCK:(hh + 1) * BLOCK] / denom).astype(o_ref.dtype)


def _swa(q_h, k_h, v_h, pos_col, pos_row, sinks, seq):
    T = q_h.shape[1]
    d = SWA_HEAD_DIM
    tiles_per_seq = seq // SWA_TILE

    def prev_blk(i):
        return jnp.where(i % tiles_per_seq == 0, i * SWA_SUB, i * SWA_SUB - 1)

    cur = lambda i: (0, i, 0)
    prev = lambda i: (0, prev_blk(i), 0)
    return pl.pallas_call(
        functools.partial(_swa_kernel, tiles_per_seq),
        grid=(T // SWA_TILE,),
        in_specs=[
            pl.BlockSpec(memory_space=pltpu.SMEM),
            pl.BlockSpec((SWA_HEADS, SWA_TILE, d), cur),
            pl.BlockSpec((SWA_KV_HEADS, SWA_TILE, d), cur),
            pl.BlockSpec((SWA_KV_HEADS, BLOCK, d), prev),
            pl.BlockSpec((SWA_KV_HEADS, SWA_TILE, d), cur),
            pl.BlockSpec((SWA_KV_HEADS, BLOCK, d), prev),
            pl.BlockSpec((SWA_TILE, 1), lambda i: (i, 0)),
            pl.BlockSpec((1, SWA_TILE), lambda i: (0, i)),
            pl.BlockSpec((1, BLOCK), lambda i: (0, prev_blk(i))),
        ],
        out_specs=pl.BlockSpec((SWA_TILE, SWA_Q_W), lambda i: (i, 0)),
        out_shape=jax.ShapeDtypeStruct((T, SWA_Q_W), BF16),
        scratch_shapes=[
            pltpu.VMEM((SWA_KV_HEADS, SWA_TILE + BLOCK, d), BF16),
            pltpu.VMEM((SWA_KV_HEADS, SWA_TILE + BLOCK, d), BF16),
            pltpu.VMEM((1, SWA_TILE + BLOCK), F32),
        ],
        compiler_params=_params("parallel"),
        name="swa",
    )(sinks, q_h, k_h, k_h, v_h, v_h, pos_col, pos_row, pos_row)


MLA_VP = 2 * MLA_V_DIM


def _mla_prep_kernel(cq_ref, ckv_ref, kr_ref, pos_ref, inv_ref, gcq_ref, gckv_ref, wqn_ref, wqr_ref, wqs_ref,
                     wkv_ref, q_out, k_out, v_out):
    scale = MLA_QK_DIM ** -0.5 * math.log2(math.e)
    ang = pos_ref[...] * inv_ref[...]
    cos = jnp.cos(ang)
    sin = jnp.sin(ang)
    cos4 = jnp.concatenate([cos] * 4, axis=1)
    sin4 = jnp.concatenate([sin] * 4, axis=1)

    cn = _rms(cq_ref[...].astype(F32), gcq_ref[...]).astype(BF16)
    qn = _dot(cn, wqn_ref[...])
    qr = _dot(cn, wqr_ref[...])
    qs = _dot(cn, wqs_ref[...])
    qrot = qr * cos4 + qs * sin4

    kvn = _rms(ckv_ref[...].astype(F32), gckv_ref[...]).astype(BF16)
    kv = _dot(kvn, wkv_ref[...])
    kr = kr_ref[...].astype(F32)
    krot = (kr[:, :MLA_ROPE_DIM] * cos[:, :MLA_ROPE_DIM]
            + kr[:, MLA_ROPE_DIM:] * sin[:, :MLA_ROPE_DIM]).astype(BF16)

    lane = lax.broadcasted_iota(jnp.int32, (kr.shape[0], MLA_VP - MLA_V_DIM), 1)
    ones_col = jnp.where(lane == 0, 1.0, 0.0).astype(BF16)
    for h in range(MLA_HEADS):
        q_h = jnp.concatenate(
            [qn[:, h * MLA_NOPE_DIM:(h + 1) * MLA_NOPE_DIM], qrot[:, h * MLA_ROPE_DIM:(h + 1) * MLA_ROPE_DIM]],
            axis=1) * scale
        q_out[0, h] = q_h.astype(BF16)
        kn_h = kv[:, h * 256:h * 256 + MLA_NOPE_DIM].astype(BF16)
        k_out[0, h] = jnp.concatenate([kn_h, krot], axis=1)
        v_out[0, h, :, :MLA_V_DIM] = kv[:, h * 256 + MLA_NOPE_DIM:(h + 1) * 256].astype(BF16)
        v_out[0, h, :, MLA_V_DIM:] = ones_col


def _mla_prep(c_q, c_kv, k_r2, pos_col, inv128, g_cq, g_ckv, wq_n, wq_r, wq_s, w_kv, batch, seq, tm=512):
    T = c_q.shape[0]
    tps = seq // tm
    const = lambda i: (0, 0)
    out_idx = lambda i: (i // tps, 0, i % tps, 0)
    return pl.pallas_call(
        _mla_prep_kernel,
        grid=(T // tm,),
        in_specs=[
            pl.BlockSpec((tm, MLA_Q_RANK), lambda i: (i, 0)),
            pl.BlockSpec((tm, MLA_KV_RANK), lambda i: (i, 0)),
            pl.BlockSpec((tm, 2 * MLA_ROPE_DIM), lambda i: (i, 0)),
            pl.BlockSpec((tm, 1), lambda i: (i, 0)),
            pl.BlockSpec((1, 128), const),
            pl.BlockSpec((1, MLA_Q_RANK), const),
            pl.BlockSpec((1, MLA_KV_RANK), const),
            pl.BlockSpec(wq_n.shape, const),
            pl.BlockSpec(wq_r.shape, const),
            pl.BlockSpec(wq_s.shape, const),
            pl.BlockSpec(w_kv.shape, const),
        ],
        out_specs=[
            pl.BlockSpec((1, MLA_HEADS, tm, MLA_QK_DIM), out_idx),
            pl.BlockSpec((1, MLA_HEADS, tm, MLA_QK_DIM), out_idx),
            pl.BlockSpec((1, MLA_HEADS, tm, MLA_VP), out_idx),
        ],
        out_shape=[
            jax.ShapeDtypeStruct((batch, MLA_HEADS, seq, MLA_QK_DIM), BF16),
            jax.ShapeDtypeStruct((batch, MLA_HEADS, seq, MLA_QK_DIM), BF16),
            jax.ShapeDtypeStruct((batch, MLA_HEADS, seq, MLA_VP), BF16),
        ],
        compiler_params=_params("parallel"),
        name="mla_prep",
    )(c_q, c_kv, k_r2, pos_col, inv128, g_cq, g_ckv, wq_n, wq_r, wq_s, w_kv)


MLA_TILE = 512
MLA_HP = 2


def _mla_flash_kernel(q_ref, k_ref, v_ref, o_ref, s_sc, p_sc, alpha_sc, m_sc, acc_sc):
    qi = pl.program_id(2)
    t = q_ref.shape[2]
    heads = range(MLA_HP)

    def scores(hh, j):
        r0 = pl.multiple_of(j * t, t)
        return _dot_nt(q_ref[0, hh], k_ref[0, hh, pl.ds(r0, t), :])

    def softmax(hh, s, slot):
        m_prev = m_sc[hh]
        m_new = jnp.maximum(m_prev, jnp.max(s, axis=1, keepdims=True))
        alpha_sc[slot, hh] = jnp.exp2(m_prev - m_new)
        p_sc[slot, hh] = jnp.exp2(s - jnp.concatenate([m_new] * (t // 128), axis=1)).astype(BF16)
        m_sc[hh] = m_new

    def accumulate(hh, j, slot):
        r0 = pl.multiple_of(j * t, t)
        pv = _dot(p_sc[slot, hh], v_ref[0, hh, pl.ds(r0, t), :])
        acc_sc[hh] = jnp.concatenate([alpha_sc[slot, hh]] * (MLA_VP // 128), axis=1) * acc_sc[hh] + pv

    m_sc[...] = jnp.full(m_sc.shape, NEG, F32)
    acc_sc[...] = jnp.zeros(acc_sc.shape, F32)
    p_sc[1] = jnp.zeros(p_sc.shape[1:], BF16)
    alpha_sc[1] = jnp.ones(alpha_sc.shape[1:], F32)
    for hh in heads:
        s_sc[0, hh] = scores(hh, 0)

    def stage(j, slot):
        for hh in heads:
            s_sc[1 - slot, hh] = scores(hh, j + 1)
            softmax(hh, s_sc[slot, hh], slot)
            accumulate(hh, jnp.maximum(j - 1, 0), 1 - slot)

    def body(i, carry):
        for k in range(4):
            stage(4 * i + k, k % 2)
        return carry

    lax.fori_loop(0, qi // 4, body, 0)

    @pl.when(qi % 4 >= 2)
    def _():
        base = (qi // 4) * 4
        stage(base, 0)
        stage(base + 1, 1)

    def finish(slot):
        row = lax.broadcasted_iota(jnp.int32, (t, t), 0)
        col = lax.broadcasted_iota(jnp.int32, (t, t), 1)
        for hh in heads:
            softmax(hh, jnp.where(col <= row, s_sc[slot, hh], NEG), slot)
            accumulate(hh, jnp.maximum(qi - 1, 0), 1 - slot)
        for hh in heads:
            accumulate(hh, qi, slot)
            acc = acc_sc[hh]
            o_ref[0, :, hh * MLA_V_DIM:(hh + 1) * MLA_V_DIM] = (
                acc[:, :MLA_V_DIM] / acc[:, MLA_V_DIM:MLA_V_DIM + 1]).astype(o_ref.dtype)

    @pl.when(qi % 2 == 0)
    def _():
        finish(0)

    @pl.when(qi % 2 == 1)
    def _():
        stage(qi - 1, 0)
        finish(1)


def _mla_flash(q, k, v):
    B, H, S, _ = q.shape
    t = MLA_TILE
    hp = MLA_HP
    return pl.pallas_call(
        _mla_flash_kernel,
        grid=(B, H // hp, S // t),
        in_specs=[
            pl.BlockSpec((1, hp, t, MLA_QK_DIM), lambda b, h, i: (b, h, i, 0)),
            pl.BlockSpec((1, hp, S, MLA_QK_DIM), lambda b, h, i: (b, h, 0, 0)),
            pl.BlockSpec((1, hp, S, MLA_VP), lambda b, h, i: (b, h, 0, 0)),
        ],
        out_specs=pl.BlockSpec((1, t, hp * MLA_V_DIM), lambda b, h, i: (b, i, h)),
        out_shape=jax.ShapeDtypeStruct((B, S, MLA_V_W), BF16),
        scratch_shapes=[
            pltpu.VMEM((2, hp, t, t), F32),
            pltpu.VMEM((2, hp, t, t), BF16),
            pltpu.VMEM((2, hp, t, 128), F32),
            pltpu.VMEM((hp, t, 128), F32),
            pltpu.VMEM((hp, t, MLA_VP), F32),
        ],
        compiler_params=_params("parallel", "parallel", "arbitrary"),
        name="mla_flash",
    )(q, k, v)


def _mix_kernel(x_ref, g_ref, ya_in_ref, yb_in_ref, wga_ref, wgb_ref, wa_ref, wb_ref, wo_ref, o_ref, h_sc, acc_sc):
    n = pl.program_id(1)

    @pl.when(n == 0)
    def _():
        h_sc[...] = _rms(x_ref[...], g_ref[...]).astype(BF16)
        acc_sc[...] = jnp.zeros(acc_sc.shape, F32)

    h = h_sc[...]
    ga = _dot_nt(h, wga_ref[...])
    gb = _dot_nt(h, wgb_ref[...])
    ya = _dot(ya_in_ref[...], wa_ref[...])
    yb = _dot(yb_in_ref[...], wb_ref[...])
    mixed = jax.nn.sigmoid(ga) * ya + jax.nn.sigmoid(gb) * yb
    acc_sc[...] += _dot(mixed.astype(BF16), wo_ref[...])

    @pl.when(n == pl.num_programs(1) - 1)
    def _():
        o_ref[...] = x_ref[...] + acc_sc[...]


def _mix(x2, g, swa_out, mla_out, w_ga, w_gb, w_a, w_b, w_o, tm=512, tn=512):
    T, D = x2.shape
    return pl.pallas_call(
        _mix_kernel,
        grid=(T // tm, D // tn),
        in_specs=[
            pl.BlockSpec((tm, D), lambda i, n: (i, 0)),
            pl.BlockSpec((1, D), lambda i, n: (0, 0)),
            pl.BlockSpec((tm, SWA_Q_W), lambda i, n: (i, 0)),
            pl.BlockSpec((tm, MLA_V_W), lambda i, n: (i, 0)),
            pl.BlockSpec((tn, D), lambda i, n: (n, 0)),
            pl.BlockSpec((tn, D), lambda i, n: (n + D // tn, 0)),
            pl.BlockSpec((SWA_Q_W, tn), lambda i, n: (0, n)),
            pl.BlockSpec((MLA_V_W, tn), lambda i, n: (0, n)),
            pl.BlockSpec((tn, D), lambda i, n: (n, 0)),
        ],
        out_specs=pl.BlockSpec((tm, D), lambda i, n: (i, 0)),
        out_shape=jax.ShapeDtypeStruct((T, D), F32),
        scratch_shapes=[pltpu.VMEM((tm, D), BF16), pltpu.VMEM((tm, D), F32)],
        compiler_params=_params("parallel", "arbitrary"),
        name="mix",
    )(x2, g, swa_out, mla_out, w_ga, w_gb, w_a, w_b, w_o)


PEER_LANES = 128
CAND_ROWS = 16 + 7 * 8 + 8


def _extract_top(s, iters, exact):
    n_rows = s.shape[0]
    rows = lax.broadcasted_iota(jnp.int32, s.shape, 0)
    rank = jnp.full(s.shape, float(iters), F32)
    vals = []
    for r in range(iters):
        m = jnp.max(s, axis=0, keepdims=True)
        if exact:
            idx = jnp.min(jnp.where(s == m, rows, n_rows), axis=0, keepdims=True)
            sel = rows == idx
        else:
            sel = s == m
        rank = jnp.where(sel, float(r), rank)
        s = jnp.where(sel, -jnp.inf, s)
        vals.append(m)
    return rank, vals


def _ranked_excess(rank, iters):
    return jnp.sum(jnp.where(rank < float(iters), 1.0, 0.0), axis=0, keepdims=True) - float(iters)


def _peer_route_kernel(x_ref, g_ref, wq_ref, keys_ref, h_out, r2_out, n1_out, e1_out, e2_out, q_sc):
    tm = x_ref.shape[0]
    h = _rms(x_ref[...], g_ref[...]).astype(BF16)
    h_out[...] = h
    q_sc[...] = _dot(h, wq_ref[...]).astype(BF16)

    def route(hd, t0, exact):
        c0 = pl.multiple_of(hd * 2 * PEER_HALF, 2 * PEER_HALF)
        q1 = q_sc[t0:t0 + PEER_LANES, pl.ds(c0, PEER_HALF)]
        q2 = q_sc[t0:t0 + PEER_LANES, pl.ds(c0 + PEER_HALF, PEER_HALF)]
        s1 = _dot_nt(keys_ref[2 * hd], q1)
        s2 = _dot_nt(keys_ref[2 * hd + 1], q2)
        r1, v1 = _extract_top(s1, PEER_TOPK, exact)
        r2, v2 = _extract_top(s2, PEER_TOPK, exact)
        v2_lo = jnp.concatenate(v2[:8], axis=0)
        v2_all = jnp.concatenate(v2, axis=0)
        slabs = [v1[0] + v2_all]
        for a in range(1, 8):
            slabs.append(v1[a] + v2_lo)
        slabs.append(jnp.concatenate(v1[8:], axis=0) + v2[0])
        cand = jnp.concatenate(slabs, axis=0)
        rc, vc = _extract_top(cand, PEER_TOPK, exact)
        picked = jnp.where(rc < float(PEER_TOPK), 1.0, 0.0)
        z = jnp.ones_like(vc[0])
        for r in range(1, PEER_TOPK):
            z = z + jnp.exp(vc[r] - vc[0])
        cnt_lo = picked[0:8]
        for a in range(1, 8):
            cnt_lo = cnt_lo + picked[8 + 8 * a:16 + 8 * a]
        first_col = jnp.sum(picked[72:80], axis=0, keepdims=True)
        cnt_hi = jnp.sum(picked[8:16], axis=0, keepdims=True)
        n1 = jnp.where(r1 == 0.0, cnt_hi, 0.0)
        for b in range(8):
            cnt_b = cnt_lo[b:b + 1] + first_col if b == 0 else cnt_lo[b:b + 1]
            n1 = n1 + jnp.where(r1 < cnt_b, 1.0, 0.0)
        r2_out[hd, :, t0:t0 + PEER_LANES] = r2.astype(r2_out.dtype)
        n1_out[hd, :, t0:t0 + PEER_LANES] = n1
        e1_out[hd, :, t0:t0 + PEER_LANES] = jnp.exp(s1 - v1[0])
        e2_out[hd, :, t0:t0 + PEER_LANES] = (jnp.exp(s2 - v2[0]) / z).astype(e2_out.dtype)
        if exact:
            return None
        excess = (_ranked_excess(r1, PEER_TOPK) + _ranked_excess(r2, PEER_TOPK)
                  + _ranked_excess(rc, PEER_TOPK))
        return jnp.max(excess)

    def head_body(hd, carry):
        starts = [ch * PEER_LANES for ch in range(tm // PEER_LANES)]
        ties = [route(hd, t0, exact=False) for t0 in starts]

        @pl.when(functools.reduce(jnp.maximum, ties) > 0.0)
        def _():
            for t0 in starts:
                route(hd, t0, exact=True)
        return carry

    lax.fori_loop(0, PEER_HEADS, head_body, 0)


def _peer_route(x1, g, w_q, keys, tm=512):
    T, D = x1.shape
    side = jax.ShapeDtypeStruct((PEER_HEADS, PEER_N_KEYS, T), F32)
    side_bf = jax.ShapeDtypeStruct((PEER_HEADS, PEER_N_KEYS, T), BF16)
    side_spec = pl.BlockSpec((PEER_HEADS, PEER_N_KEYS, tm), lambda i: (0, 0, i))
    return pl.pallas_call(
        _peer_route_kernel,
        grid=(T // tm,),
        in_specs=[
            pl.BlockSpec((tm, D), lambda i: (i, 0)),
            pl.BlockSpec((1, D), lambda i: (0, 0)),
            pl.BlockSpec(w_q.shape, lambda i: (0, 0)),
            pl.BlockSpec(keys.shape, lambda i: (0, 0, 0)),
        ],
        out_specs=[pl.BlockSpec((tm, D), lambda i: (i, 0)), side_spec, side_spec, side_spec, side_spec],
        out_shape=[jax.ShapeDtypeStruct((T, D), BF16), side_bf, side, side, side_bf],
        scratch_shapes=[pltpu.VMEM((tm, w_q.shape[1]), BF16)],
        compiler_params=_params("parallel"),
        name="peer_route",
    )(x1, g, w_q, keys)


PEER_RES_ROWS = 256


def _peer_dense_kernel(last_layer, h_ref, u_ref, vt_ref, r2_ref, n1_ref, e1_ref, e2_ref, x_hbm, g_ref, o_ref,
                       acc_sc, x_buf, x_sem):
    e = pl.program_id(1)
    te = u_ref.shape[0]

    @pl.when(e == 0)
    def _():
        acc_sc[...] = jnp.zeros(acc_sc.shape, F32)

    a_t = _dot_nt(u_ref[...].astype(BF16), h_ref[...])
    act = (0.5 * a_t * (1.0 + lax.erf(a_t * math.sqrt(0.5)))).astype(BF16)
    zero = jnp.zeros((), BF16)
    w_rows = []
    for ii in range(te // PEER_N_KEYS):
        i = e * (te // PEER_N_KEYS) + ii
        gate = None
        for hd in range(PEER_HEADS):
            n1 = n1_ref[hd, pl.ds(i, 1), :].astype(BF16)
            e1 = e1_ref[hd, pl.ds(i, 1), :].astype(BF16)
            term = jnp.where(r2_ref[hd] < n1, e2_ref[hd] * e1, zero)
            gate = term if gate is None else gate + term
        w_rows.append(act[ii * PEER_N_KEYS:(ii + 1) * PEER_N_KEYS] * gate)
    w = jnp.concatenate(w_rows, axis=0) if len(w_rows) > 1 else w_rows[0]
    acc_sc[...] += _dot(vt_ref[...].T, w)

    @pl.when(e == pl.num_programs(1) - 1)
    def _():
        tm = h_ref.shape[0]
        rc = PEER_RES_ROWS
        row0 = pl.program_id(0) * tm

        def fetch(c):
            return pltpu.make_async_copy(x_hbm.at[pl.ds(row0 + c * rc, rc), :], x_buf.at[c % 2], x_sem.at[c % 2])

        n_chunks = tm // rc
        fetch(0).start()
        for c in range(n_chunks):
            if c + 1 < n_chunks:
                fetch(c + 1).start()
            fetch(c).wait()
            y = x_buf[c % 2] + acc_sc[:, c * rc:(c + 1) * rc].T
            o_ref[c * rc:(c + 1) * rc, :] = _rms(y, g_ref[...]) if last_layer else y


def _peer_dense(h2, u_bf, vt_bf, r2, n1, e1, e2, x1, g_out, last_layer, tm=1024, te=512):
    T, D = h2.shape
    once = pl.Buffered(1)
    side = (PEER_HEADS, PEER_N_KEYS, tm)
    side_idx = lambda i, e: (0, 0, i)
    return pl.pallas_call(
        functools.partial(_peer_dense_kernel, last_layer),
        grid=(T // tm, PEER_N_EXPERTS // te),
        in_specs=[
            pl.BlockSpec((tm, D), lambda i, e: (i, 0), pipeline_mode=once),
            pl.BlockSpec((te, D), lambda i, e: (e, 0)),
            pl.BlockSpec((te, D), lambda i, e: (e, 0)),
            pl.BlockSpec(side, side_idx, pipeline_mode=once),
            pl.BlockSpec(side, side_idx, pipeline_mode=once),
            pl.BlockSpec(side, side_idx, pipeline_mode=once),
            pl.BlockSpec(side, side_idx, pipeline_mode=once),
            pl.BlockSpec(memory_space=pl.ANY),
            pl.BlockSpec((1, D), lambda i, e: (0, 0)),
        ],
        out_specs=pl.BlockSpec((tm, D), lambda i, e: (i, 0), pipeline_mode=once),
        out_shape=jax.ShapeDtypeStruct((T, D), F32),
        scratch_shapes=[
            pltpu.VMEM((D, tm), F32),
            pltpu.VMEM((2, PEER_RES_ROWS, D), F32),
            pltpu.SemaphoreType.DMA((2,)),
        ],
        compiler_params=_params("arbitrary", "arbitrary"),
        name="peer_dense",
    )(h2, u_bf, vt_bf, r2, n1, e1, e2, x1, g_out)


def _layer(x2, pos_col, pos_row, batch, seq, g_mix, w_in, sinks, g_cq, w_uq, g_ckv, w_ukv, w_a_proj, w_b_proj,
           w_o, g_ffn, w_peer_q, peer_keys, peer_u, peer_v, g_out, last_layer):
    D = x2.shape[1]
    o = np.cumsum([0, SWA_Q_W, SWA_KV_W, SWA_KV_W, MLA_Q_RANK, MLA_KV_RANK, MLA_ROPE_DIM, D, D])
    w_in_t = w_in.T
    w_gates_t = w_in_t[o[6]:o[8]].astype(BF16)
    half = MLA_ROPE_DIM // 2

    q_h, c_q, k_h, v_h, c_kv, k_r2 = _attn_inproj(x2, g_mix.reshape(1, D), w_in_t)
    swa_out = _swa(q_h, k_h, v_h, pos_col, pos_row, sinks.astype(F32), seq)

    w_uq3 = w_uq.reshape(MLA_Q_RANK, MLA_HEADS, MLA_QK_DIM)
    wq_n = w_uq3[:, :, :MLA_NOPE_DIM].reshape(MLA_Q_RANK, MLA_HEADS * MLA_NOPE_DIM).astype(BF16)
    t1 = w_uq3[:, :, MLA_NOPE_DIM:MLA_NOPE_DIM + half]
    t2 = w_uq3[:, :, MLA_NOPE_DIM + half:]
    wq_r = jnp.concatenate([t1, t2], axis=2).reshape(MLA_Q_RANK, MLA_HEADS * MLA_ROPE_DIM).astype(BF16)
    wq_s = jnp.concatenate([-t2, t1], axis=2).reshape(MLA_Q_RANK, MLA_HEADS * MLA_ROPE_DIM).astype(BF16)
    inv = 1.0 / (ROPE_THETA ** (jnp.arange(0, MLA_ROPE_DIM, 2, dtype=F32) / MLA_ROPE_DIM))
    inv128 = jnp.tile(inv, 128 // half).reshape(1, 128)
    q, k, v = _mla_prep(c_q, c_kv, k_r2, pos_col, inv128, g_cq.reshape(1, -1), g_ckv.reshape(1, -1),
                        wq_n, wq_r, wq_s, w_ukv.astype(BF16), batch, seq)
    mla_out = _mla_flash(q, k, v).reshape(batch * seq, MLA_V_W)

    x1 = _mix(x2, g_mix.reshape(1, D), swa_out, mla_out, w_gates_t, w_gates_t,
              w_a_proj.astype(BF16), w_b_proj.astype(BF16), w_o.astype(BF16))

    keys = peer_keys.reshape(PEER_HEADS * 2, PEER_N_KEYS, PEER_HALF).astype(BF16)
    h2, r2, n1, e1, e2 = _peer_route(x1, g_ffn.reshape(1, D), w_peer_q.astype(BF16), keys)
    return _peer_dense(h2, peer_u, peer_v.astype(BF16), r2, n1, e1, e2, x1, g_out.reshape(1, D), last_layer)


def kernel(x, positions, g_mix, w_in, sinks, g_cq, w_uq, g_ckv, w_ukv, w_a_proj, w_b_proj, w_o, g_ffn, w_peer_q,
           peer_keys, peer_u, peer_v, g_final):
    batch, seq, D = x.shape
    depth = g_mix.shape[0]
    assert seq % 1024 == 0 and D % 512 == 0, (seq, D)
    x2 = x.reshape(batch * seq, D)
    pos_f = positions.astype(F32)
    pos_col = pos_f.reshape(batch * seq, 1)
    pos_row = pos_f.reshape(1, batch * seq)
    for l in range(depth):
        x2 = _layer(x2, pos_col, pos_row, batch, seq, g_mix[l], w_in[l], sinks[l], g_cq[l], w_uq[l],
                    g_ckv[l], w_ukv[l], w_a_proj[l], w_b_proj[l], w_o[l], g_ffn[l], w_peer_q[l],
                    peer_keys[l], peer_u[l], peer_v[l], g_final, l == depth - 1)
    return x2.reshape(batch, seq, D)
```
